```python
import jax, jax.numpy as jnp
from jax import lax
import numpy as np

D_MODEL = 1024
BATCH = 1
SEQ = 16384
DEPTH = 1
DEC_BATCH = 4
DEC_SEQ = 8192
PAST_LEN = 128

GRID_W = 64
MIX_WIDTH = D_MODEL
NA_WIDTH = MIX_WIDTH // 2
RET_WIDTH = MIX_WIDTH - NA_WIDTH
NA_HEAD_DIM = 64
NA_HEADS = NA_WIDTH // NA_HEAD_DIM
RET_HEAD_DIM = 128
RET_HEADS = RET_WIDTH // RET_HEAD_DIM
NA_WIN_R = 8
NA_WIN_C = 16
NA_QBLOCK_C = 16
NA_BAND_C = 32
RET_CHUNK = 128
ROPE_BASE = 10000.0
FFN_DIM = 2816
CONV_W = 3
NORM_EPS = 1e-6
IN_PROJ_WIDTH = 3 * NA_WIDTH + 4 * RET_WIDTH

kernel_name = "hybrid_natten_retnet_convffn_encoder"


def rms_norm(x, w):
    xf = x.astype(jnp.float32)
    y = xf * lax.rsqrt(jnp.mean(xf * xf, axis=-1, keepdims=True) + NORM_EPS)
    return (y * w.astype(jnp.float32)).astype(x.dtype)


def rotary(x):
    L, D = x.shape[1], x.shape[-1]
    half = D // 2
    inv_freq = ROPE_BASE ** (-jnp.arange(half, dtype=jnp.float32) / half)
    ang = jnp.arange(L, dtype=jnp.float32)[:, None] * inv_freq[None, :]
    cos = jnp.cos(ang)[None, :, None, :]
    sin = jnp.sin(ang)[None, :, None, :]
    xf = x.astype(jnp.float32)
    x1, x2 = xf[..., :half], xf[..., half:]
    return jnp.concatenate([x1 * cos - x2 * sin, x2 * cos + x1 * sin], axis=-1)


def neighbourhood_attention(q, k, v, rpb):
    B, L, H, Dh = q.shape
    rows = L // GRID_W
    kr = min(NA_WIN_R, rows)
    qg = q.reshape(B, rows, GRID_W, H, Dh)
    kg = k.reshape(B, rows, GRID_W, H, Dh)
    vg = v.reshape(B, rows, GRID_W, H, Dh)
    r = jnp.arange(rows)
    row_start = jnp.clip(r - kr // 2, 0, rows - kr)
    key_rows = row_start[:, None] + jnp.arange(kr)[None, :]
    k_band = kg[:, key_rows]
    v_band = vg[:, key_rows]
    dr_idx = key_rows - r[:, None] + (NA_WIN_R - 1)
    scale = Dh ** -0.5
    outs = []
    for c0 in range(0, GRID_W, NA_QBLOCK_C):
        b0 = min(max(c0 - NA_WIN_C // 2, 0), GRID_W - NA_BAND_C)
        qc = jnp.arange(c0, c0 + NA_QBLOCK_C)
        kc = jnp.arange(b0, b0 + NA_BAND_C)
        col_start = jnp.clip(qc - NA_WIN_C // 2, 0, GRID_W - NA_WIN_C)
        col_mask = (kc[None, :] >= col_start[:, None]) & (kc[None, :] < col_start[:, None] + NA_WIN_C)
        dc_idx = jnp.clip(kc[None, :] - qc[:, None] + (NA_WIN_C - 1), 0, 2 * NA_WIN_C - 2)
        bias = rpb.astype(jnp.float32)[:, dr_idx[:, None, :, None], dc_idx[None, :, None, :]]
        bias = jnp.transpose(bias, (1, 0, 2, 3, 4))
        q_blk = qg[:, :, c0:c0 + NA_QBLOCK_C]
        k_blk = k_band[:, :, :, b0:b0 + NA_BAND_C]
        v_blk = v_band[:, :, :, b0:b0 + NA_BAND_C]
        s = jnp.einsum('brqhd,brkchd->brhqkc', q_blk, k_blk).astype(jnp.float32) * scale + bias[None]
        s = jnp.where(col_mask[:, None, :], s, -jnp.inf)
        p = jax.nn.softmax(s.reshape(B, rows, H, NA_QBLOCK_C, kr * NA_BAND_C), axis=-1)
        p = p.reshape(B, rows, H, NA_QBLOCK_C, kr, NA_BAND_C).astype(v.dtype)
        outs.append(jnp.einsum('brhqkc,brkchd->brqhd', p, v_blk))
    o = jnp.concatenate(outs, axis=2)
    return o.reshape(B, L, H * Dh)


def retention_one_direction(q, k, v, log_gamma, include_diag):
    B, H, L, D = q.shape
    C = RET_CHUNK
    n = L // C
    qc = q.reshape(B, H, n, C, D)
    kc = k.reshape(B, H, n, C, D)
    vc = v.reshape(B, H, n, C, D)
    i = jnp.arange(C, dtype=jnp.float32)
    diff = i[:, None] - i[None, :]
    mask = (diff >= 0) if include_diag else (diff > 0)
    decay_intra = jnp.where(mask[None], jnp.exp(log_gamma[:, None, None] * jnp.maximum(diff, 0.0)[None]), 0.0)
    decay_q = jnp.exp(log_gamma[:, None] * (i + 1.0)[None])
    decay_k = jnp.exp(log_gamma[:, None] * (C - 1.0 - i)[None])
    decay_chunk = jnp.exp(log_gamma * C)
    s = jnp.einsum('bhncd,bhnmd->bhncm', qc, kc) * decay_intra[None, :, None]
    intra = jnp.einsum('bhncm,bhnme->bhnce', s, vc)
    kv = jnp.einsum('bhncd,bhnce->bhnde', kc * decay_k[None, :, None, :, None], vc)

    def step(state, kv_n):
        return state * decay_chunk[None, :, None, None] + kv_n, state

    _, states = lax.scan(step, jnp.zeros((B, H, D, D), jnp.float32), jnp.moveaxis(kv, 2, 0))
    states = jnp.moveaxis(states, 0, 2)
    cross = jnp.einsum('bhncd,bhnde->bhnce', qc * decay_q[None, :, None, :, None], states)
    return (intra + cross).reshape(B, H, L, D)


def bidirectional_retention(q, k, v, g, theta_fwd, theta_bwd, norm_w):
    B, L, H, D = q.shape
    out_dtype = v.dtype
    qr = jnp.transpose(rotary(q), (0, 2, 1, 3))
    kr = jnp.transpose(rotary(k) * (D ** -0.5), (0, 2, 1, 3))
    vr = jnp.transpose(v.astype(jnp.float32), (0, 2, 1, 3))
    lg_f = jax.nn.log_sigmoid(theta_fwd.astype(jnp.float32))
    lg_b = jax.nn.log_sigmoid(theta_bwd.astype(jnp.float32))
    fwd = retention_one_direction(qr, kr, vr, lg_f, True)
    bwd = jnp.flip(retention_one_direction(jnp.flip(qr, 2), jnp.flip(kr, 2), jnp.flip(vr, 2), lg_b, False), 2)
    y = jnp.transpose(fwd + bwd, (0, 2, 1, 3))
    y = rms_norm(y, norm_w.reshape(H, D))
    y = jax.nn.silu(g.astype(jnp.float32)) * y
    return y.reshape(B, L, H * D).astype(out_dtype)


def conv_ffn(x, w_up, conv_w, conv_b, w_down):
    h = x @ w_up
    ch = h.shape[-1]
    h = lax.conv_general_dilated(h, conv_w[:, None, :].astype(h.dtype), window_strides=(1,),
                                 padding=((CONV_W // 2, CONV_W // 2),),
                                 dimension_numbers=('NWC', 'WIO', 'NWC'),
                                 feature_group_count=ch) + conv_b
    g, u = jnp.split(h, 2, axis=-1)
    return (jax.nn.silu(g) * u) @ w_down


def encoder_layer(x, norm1_w, w_in, na_q_norm_w, na_k_norm_w, na_rpb, ret_theta_fwd, ret_theta_bwd,
                  ret_norm_w, w_out, norm2_w, w_ffn_up, ffn_conv_w, ffn_conv_b, w_ffn_down):
    B, L, _ = x.shape
    h = rms_norm(x, norm1_w)
    proj = h @ w_in
    cuts = np.cumsum([NA_WIDTH, NA_WIDTH, NA_WIDTH, RET_WIDTH, RET_WIDTH, RET_WIDTH]).tolist()
    na_q, na_k, na_v, r_q, r_k, r_v, r_g = jnp.split(proj, cuts, axis=-1)
    na_q = rms_norm(na_q.reshape(B, L, NA_HEADS, NA_HEAD_DIM), na_q_norm_w)
    na_k = rms_norm(na_k.reshape(B, L, NA_HEADS, NA_HEAD_DIM), na_k_norm_w)
    na_v = na_v.reshape(B, L, NA_HEADS, NA_HEAD_DIM)
    na_out = neighbourhood_attention(na_q, na_k, na_v, na_rpb)
    rs = (B, L, RET_HEADS, RET_HEAD_DIM)
    ret_out = bidirectional_retention(r_q.reshape(rs), r_k.reshape(rs), r_v.reshape(rs), r_g.reshape(rs),
                                      ret_theta_fwd, ret_theta_bwd, ret_norm_w)
    mix = jnp.concatenate([na_out.astype(x.dtype), ret_out.astype(x.dtype)], axis=-1)
    x = x + mix @ w_out
    x = x + conv_ffn(rms_norm(x, norm2_w), w_ffn_up, ffn_conv_w, ffn_conv_b, w_ffn_down)
    return x


def setup_inputs(seed: int = 0) -> dict:
    key = jax.random.key(seed)
    ks = jax.random.split(key, 16)
    f32 = jnp.float32
    base_logit = jnp.asarray(np.log(2.0 ** (5.0 + np.arange(RET_HEADS)) - 1.0).astype(np.float32))
    return {
        "x_prompt": jax.random.normal(ks[0], (BATCH, SEQ, D_MODEL), f32),
        "x_sample": jax.random.normal(ks[1], (DEC_BATCH, DEC_SEQ, D_MODEL), f32),
        "norm1_w": 1.0 + 0.02 * jax.random.normal(ks[2], (DEPTH, D_MODEL), f32),
        "w_in": jax.random.normal(ks[3], (DEPTH, D_MODEL, IN_PROJ_WIDTH), f32) * D_MODEL ** -0.5,
        "na_q_norm_w": 1.0 + 0.02 * jax.random.normal(ks[4], (DEPTH, NA_HEAD_DIM), f32),
        "na_k_norm_w": 1.0 + 0.02 * jax.random.normal(ks[5], (DEPTH, NA_HEAD_DIM), f32),
        "na_rpb": 0.1 * jax.random.normal(ks[6], (DEPTH, NA_HEADS, 2 * NA_WIN_R - 1, 2 * NA_WIN_C - 1), f32),
        "ret_theta_fwd": base_logit[None] + 0.05 * jax.random.normal(ks[7], (DEPTH, RET_HEADS), f32),
        "ret_theta_bwd": base_logit[None] + 0.05 * jax.random.normal(ks[8], (DEPTH, RET_HEADS), f32),
        "ret_norm_w": 1.0 + 0.02 * jax.random.normal(ks[9], (DEPTH, RET_WIDTH), f32),
        "w_out": jax.random.normal(ks[10], (DEPTH, MIX_WIDTH, D_MODEL), f32) * MIX_WIDTH ** -0.5,
        "norm2_w": 1.0 + 0.02 * jax.random.normal(ks[11], (DEPTH, D_MODEL), f32),
        "w_ffn_up": jax.random.normal(ks[12], (DEPTH, D_MODEL, 2 * FFN_DIM), f32) * D_MODEL ** -0.5,
        "ffn_conv_w": jax.random.normal(ks[13], (DEPTH, CONV_W, 2 * FFN_DIM), f32) * CONV_W ** -0.5,
        "ffn_conv_b": 0.01 * jax.random.normal(ks[14], (DEPTH, 2 * FFN_DIM), f32),
        "w_ffn_down": jax.random.normal(ks[15], (DEPTH, FFN_DIM, D_MODEL), f32) * FFN_DIM ** -0.5,
    }


def reference(x_prompt, x_sample, norm1_w, w_in, na_q_norm_w, na_k_norm_w, na_rpb, ret_theta_fwd,
              ret_theta_bwd, ret_norm_w, w_out, norm2_w, w_ffn_up, ffn_conv_w, ffn_conv_b, w_ffn_down):
    def run(x):
        for l in range(DEPTH):
            x = encoder_layer(x, norm1_w[l], w_in[l], na_q_norm_w[l], na_k_norm_w[l], na_rpb[l],
                              ret_theta_fwd[l], ret_theta_bwd[l], ret_norm_w[l], w_out[l], norm2_w[l],
                              w_ffn_up[l], ffn_conv_w[l], ffn_conv_b[l], w_ffn_down[l])
        return x

    y_prompt = run(x_prompt)
    y_sample = run(x_sample)
    return (y_prompt, y_sample)
```

```python
import functools

import jax
import jax.numpy as jnp
import numpy as np
from jax import lax
from jax.experimental import pallas as pl
from jax.experimental.pallas import tpu as pltpu

F32 = jnp.float32
BF16 = jnp.bfloat16

GRID_W = 64
NA_HEADS = 8
NA_HEAD_DIM = 64
NA_WIDTH = NA_HEADS * NA_HEAD_DIM
NA_WIN_R = 8
NA_WIN_C = 16
RET_HEADS = 4
RET_HEAD_DIM = 128
RET_WIDTH = RET_HEADS * RET_HEAD_DIM
RET_CHUNK = 128
ROPE_BASE = 10000.0
CONV_W = 3
NORM_EPS = 1e-6

LANES = 128
BF16_ROWS = 16
TOK_TILE = 512
NA_ROWS_PER_STEP = TOK_TILE // GRID_W
NA_GROUP_ROWS = 4
NA_KEY_ROWS = 12
NA_GROUP_TOK = NA_GROUP_ROWS * GRID_W
NA_KEY_TOK = NA_KEY_ROWS * GRID_W
MASK_VALUE = -1e30
FFN_CHUNK = 256
HALO = BF16_ROWS
VMEM_LIMIT = 56 * 1024 * 1024


def _const_spec(shape):
    return pl.BlockSpec(shape, lambda *_: (0,) * len(shape), pipeline_mode=pl.Buffered(1))


def _rms(x, w):
    ms = jnp.mean(x * x, axis=-1, keepdims=True)
    return x * lax.rsqrt(ms + NORM_EPS) * w


def _in_proj_kernel(x_ref, n1_ref, w_ref, bd_ref, qn_ref, kn_ref, cs_ref, sn_ref,
                    naq_ref, nak_ref, nav_ref, rq_ref, rk_ref, rv_ref, rg_ref):
    h = _rms(x_ref[...], n1_ref[...]).astype(BF16)

    def proj(idx):
        lo = idx * NA_WIDTH
        return jnp.dot(h, w_ref[:, lo:lo + NA_WIDTH], preferred_element_type=F32)

    def head_norm(t, w_row, scale):
        ms = jnp.dot((t * t).astype(BF16), bd_ref[...], preferred_element_type=F32)
        return t * lax.rsqrt(ms + NORM_EPS) * w_row * scale

    naq_ref[...] = head_norm(proj(0), qn_ref[...], NA_HEAD_DIM ** -0.5).astype(BF16)
    nak_ref[...] = head_norm(proj(1), kn_ref[...], 1.0).astype(BF16)
    nav_ref[...] = proj(2).astype(BF16)

    cs = cs_ref[...]
    sn = sn_ref[...]

    def rotary(t, scale):
        outs = []
        for hh in range(RET_HEADS):
            th = t[:, hh * LANES:(hh + 1) * LANES]
            r = th * cs + pltpu.roll(th, RET_HEAD_DIM // 2, 1) * sn
            outs.append(r * scale if scale != 1.0 else r)
        return jnp.concatenate(outs, axis=1)

    rq_ref[...] = rotary(proj(3), 1.0).astype(BF16)
    rk_ref[...] = rotary(proj(4), RET_HEAD_DIM ** -0.5).astype(BF16)
    rv_ref[...] = proj(5).astype(BF16)
    rg_ref[...] = proj(6).astype(BF16)


def _in_proj(x2d, seq_len, n1, w_in, bd, qn, kn, cs, sn):
    n_tok, d_model = x2d.shape
    tiles_per_seq = seq_len // TOK_TILE
    out = jax.ShapeDtypeStruct((n_tok, NA_WIDTH), BF16)
    tok_spec = lambda width: pl.BlockSpec((TOK_TILE, width), lambda i: (i, 0))
    pos_spec = pl.BlockSpec((TOK_TILE, LANES), lambda i: (i % tiles_per_seq, 0))
    return pl.pallas_call(
        _in_proj_kernel,
        out_shape=(out,) * 7,
        grid=(n_tok // TOK_TILE,),
        in_specs=[tok_spec(d_model), _const_spec(n1.shape), _const_spec(w_in.shape),
                  _const_spec(bd.shape), _const_spec(qn.shape), _const_spec(kn.shape),
                  pos_spec, pos_spec],
        out_specs=(tok_spec(NA_WIDTH),) * 7,
        compiler_params=pltpu.CompilerParams(dimension_semantics=("arbitrary",),
                                             vmem_limit_bytes=VMEM_LIMIT),
        name="in_proj",
    )(x2d, n1, w_in, bd, qn, kn, cs, sn)


def _na_kernel(q_ref, kp_ref, kc_ref, kn_ref, vp_ref, vc_ref, vn_ref, bias_ref, o_ref,
               kbuf, vbuf, *, grid_rows):
    r = pl.program_id(1)
    kbuf[0:TOK_TILE, :] = kp_ref[...]
    kbuf[TOK_TILE:2 * TOK_TILE, :] = kc_ref[...]
    kbuf[2 * TOK_TILE:3 * TOK_TILE, :] = kn_ref[...]
    vbuf[0:TOK_TILE, :] = vp_ref[...]
    vbuf[TOK_TILE:2 * TOK_TILE, :] = vc_ref[...]
    vbuf[2 * TOK_TILE:3 * TOK_TILE, :] = vn_ref[...]

    lane = lax.broadcasted_iota(jnp.int32, (NA_GROUP_TOK, LANES), 1)
    low_half = lane < NA_HEAD_DIM

    for g in range(NA_ROWS_PER_STEP // NA_GROUP_ROWS):
        row0 = r * NA_ROWS_PER_STEP + g * NA_GROUP_ROWS
        is_first = row0 == 0
        is_last = row0 == grid_rows - NA_GROUP_ROWS
        cls = jnp.where(is_first, 0, jnp.where(is_last, 2, 1))
        win_row = jnp.where(is_first, 0, jnp.where(is_last, grid_rows - NA_KEY_ROWS, row0 - NA_GROUP_ROWS))
        start = pl.multiple_of((win_row - (r - 1) * NA_ROWS_PER_STEP) * GRID_W, GRID_W)
        kwin = kbuf[pl.ds(start, NA_KEY_TOK), :]
        vwin = vbuf[pl.ds(start, NA_KEY_TOK), :]
        qg = q_ref[g * NA_GROUP_TOK:(g + 1) * NA_GROUP_TOK, :]
        for p in range(NA_HEADS // 2):
            sl = slice(p * LANES, (p + 1) * LANES)
            q2 = qg[:, sl]
            k2 = kwin[:, sl]
            v2 = vwin[:, sl]
            halves = []
            for hh in range(2):
                keep = low_half if hh == 0 else jnp.logical_not(low_half)
                qm = jnp.where(keep, q2, jnp.zeros_like(q2))
                s = lax.dot_general(qm, k2, (((1,), (1,)), ((), ())), preferred_element_type=F32)
                s = s + bias_ref[cls, 2 * p + hh]
                m = jnp.max(s, axis=-1, keepdims=True)
                e = jnp.exp(s - m)
                l = jnp.sum(e, axis=-1, keepdims=True)
                o2 = jnp.dot(e.astype(BF16), v2, preferred_element_type=F32)
                halves.append(o2 / l)
            o_ref[g * NA_GROUP_TOK:(g + 1) * NA_GROUP_TOK, sl] = (
                jnp.where(low_half, halves[0], halves[1]).astype(BF16))


def _na_bias_tables(rpb):
    a = np.arange(NA_GROUP_ROWS)[:, None]
    b = np.arange(NA_KEY_ROWS)[None, :]
    rel = np.stack([b - a, b - NA_GROUP_ROWS - a, b - 2 * NA_GROUP_ROWS - a])
    row_ok = np.stack([np.broadcast_to(b < NA_WIN_R, rel[0].shape),
                       (b - a >= 0) & (b - a < NA_WIN_R),
                       np.broadcast_to(b >= NA_KEY_ROWS - NA_WIN_R, rel[0].shape)])
    dr = np.clip(rel + NA_WIN_R - 1, 0, 2 * NA_WIN_R - 2)
    c = np.arange(GRID_W)[:, None]
    kc = np.arange(GRID_W)[None, :]
    col_start = np.clip(c - NA_WIN_C // 2, 0, GRID_W - NA_WIN_C)
    col_ok = (kc >= col_start) & (kc < col_start + NA_WIN_C)
    dc = np.clip(kc - c + NA_WIN_C - 1, 0, 2 * NA_WIN_C - 2)
    vals = rpb.astype(F32)[:, dr[:, :, :, None, None], dc[None, None, None, :, :]]
    ok = row_ok[:, :, :, None, None] & col_ok[None, None, None, :, :]
    vals = jnp.where(ok[None], vals, MASK_VALUE)
    vals = jnp.transpose(vals, (1, 0, 2, 4, 3, 5))
    return vals.reshape(3, NA_HEADS, NA_GROUP_TOK, NA_KEY_TOK)


def _na(q, k, v, bias, batch, seq_len):
    n_tok = q.shape[0]
    grid_rows = seq_len // GRID_W
    nb = seq_len // TOK_TILE
    cur = lambda b, r: (b * nb + r, 0)
    prv = lambda b, r: (b * nb + jnp.maximum(r - 1, 0), 0)
    nxt = lambda b, r: (b * nb + jnp.minimum(r + 1, nb - 1), 0)
    blk = lambda f: pl.BlockSpec((TOK_TILE, NA_WIDTH), f)
    return pl.pallas_call(
        functools.partial(_na_kernel, grid_rows=grid_rows),
        out_shape=jax.ShapeDtypeStruct((n_tok, NA_WIDTH), BF16),
        grid=(batch, nb),
        in_specs=[blk(cur), blk(prv), blk(cur), blk(nxt), blk(prv), blk(cur), blk(nxt),
                  _const_spec(bias.shape)],
        out_specs=blk(cur),
        scratch_shapes=[pltpu.VMEM((3 * TOK_TILE, NA_WIDTH), BF16),
                        pltpu.VMEM((3 * TOK_TILE, NA_WIDTH), BF16)],
        compiler_params=pltpu.CompilerParams(dimension_semantics=("arbitrary", "arbitrary"),
                                             vmem_limit_bytes=VMEM_LIMIT),
        name="na",
    )(q, k, k, k, v, v, v, bias)


def _ret_kernel(q_ref, k_ref, v_ref, g_ref, lgf_ref, lgb_ref, nw_ref, o_ref, t_all, state,
                *, steps):
    phase = pl.program_id(1)
    s = pl.program_id(2)
    chunks = TOK_TILE // RET_CHUNK
    C = RET_CHUNK

    @pl.when(s == 0)
    def _():
        state[...] = jnp.zeros_like(state)

    row = lax.broadcasted_iota(jnp.int32, (C, LANES), 0).astype(F32)
    col = lax.broadcasted_iota(jnp.int32, (C, LANES), 1).astype(F32)

    @pl.when(phase == 0)
    def _():
        tile = steps - 1 - s
        for c in reversed(range(chunks)):
            rows = slice(c * C, (c + 1) * C)
            n = tile * chunks + c
            for hh in range(RET_HEADS):
                sl = slice(hh * LANES, (hh + 1) * LANES)
                lgb = lgb_ref[:, sl]
                t_all[n, hh] = state[hh].astype(BF16)
                kd = (k_ref[rows, sl].astype(F32) * jnp.exp(lgb * row)).T.astype(BF16)
                kv = jnp.dot(kd, v_ref[rows, sl], preferred_element_type=F32)
                state[hh] = state[hh] * jnp.exp(lgb * C) + kv

    @pl.when(phase == 1)
    def _():
        for c in range(chunks):
            rows = slice(c * C, (c + 1) * C)
            n = s * chunks + c
            for hh in range(RET_HEADS):
                sl = slice(hh * LANES, (hh + 1) * LANES)
                lgf = lgf_ref[:, sl]
                lgb = lgb_ref[:, sl]
                q = q_ref[rows, sl]
                k = k_ref[rows, sl]
                v = v_ref[rows, sl]
                diff = row - col
                decay = jnp.where(diff >= 0, jnp.exp(lgf * jnp.maximum(diff, 0.0)),
                                  jnp.exp(lgb * jnp.maximum(-diff, 0.0)))
                a = lax.dot_general(q, k, (((1,), (1,)), ((), ())), preferred_element_type=F32)
                y = jnp.dot((a * decay).astype(BF16), v, preferred_element_type=F32)
                y = y + jnp.exp(lgf * (row + 1.0)) * jnp.dot(
                    q, state[hh].astype(BF16), preferred_element_type=F32)
                y = y + jnp.exp(lgb * (C - row)) * jnp.dot(
                    q, t_all[n, hh], preferred_element_type=F32)
                kd = (k.astype(F32) * jnp.exp(lgf * (C - 1.0 - row))).T.astype(BF16)
                kv = jnp.dot(kd, v, preferred_element_type=F32)
                state[hh] = state[hh] * jnp.exp(lgf * C) + kv
                g = g_ref[rows, sl].astype(F32)
                gate = g * (1.0 / (1.0 + jnp.exp(-g)))
                o_ref[rows, sl] = (gate * _rms(y, nw_ref[:, sl])).astype(BF16)


def _retention(q, k, v, g, lgf, lgb, nw, batch, seq_len):
    n_tok = q.shape[0]
    steps = seq_len // TOK_TILE
    n_chunks = seq_len // RET_CHUNK
    fwd_only = lambda b, p, s: (b * steps + s * p, 0)
    both = lambda b, p, s: (b * steps + jnp.where(p == 0, steps - 1 - s, s), 0)
    blk = lambda f: pl.BlockSpec((TOK_TILE, RET_WIDTH), f)
    return pl.pallas_call(
        functools.partial(_ret_kernel, steps=steps),
        out_shape=jax.ShapeDtypeStruct((n_tok, RET_WIDTH), BF16),
        grid=(batch, 2, steps),
        in_specs=[blk(fwd_only), blk(both), blk(both), blk(fwd_only),
                  _const_spec(lgf.shape), _const_spec(lgb.shape), _const_spec(nw.shape)],
        out_specs=blk(fwd_only),
        scratch_shapes=[pltpu.VMEM((n_chunks, RET_HEADS, RET_HEAD_DIM, RET_HEAD_DIM), BF16),
                        pltpu.VMEM((RET_HEADS, RET_HEAD_DIM, RET_HEAD_DIM), F32)],
        compiler_params=pltpu.CompilerParams(
            dimension_semantics=("arbitrary", "arbitrary", "arbitrary"),
            vmem_limit_bytes=VMEM_LIMIT),
        name="retention",
    )(q, k, v, g, lgf, lgb, nw)


def _out_ffn_kernel(x_ref, xp_ref, xn_ref, na_ref, nap_ref, nan_ref, rt_ref, rtp_ref, rtn_ref,
                    wo_ref, n2_ref, wu_ref, cw_ref, cb_ref, wd_ref, o_ref, *, tiles_per_seq, ffn_dim):
    i = pl.program_id(0)
    rows = TOK_TILE + 2 * HALO
    x = jnp.concatenate([xp_ref[...], x_ref[...], xn_ref[...]], axis=0)
    na = jnp.concatenate([nap_ref[...], na_ref[...], nan_ref[...]], axis=0)
    rt = jnp.concatenate([rtp_ref[...], rt_ref[...], rtn_ref[...]], axis=0)
    x2 = (x + jnp.dot(na, wo_ref[:NA_WIDTH, :], preferred_element_type=F32)
          + jnp.dot(rt, wo_ref[NA_WIDTH:, :], preferred_element_type=F32))
    pos = i % tiles_per_seq
    ridx = lax.broadcasted_iota(jnp.int32, (rows, 1), 0)
    dead = ((ridx < HALO) & (pos == 0)) | ((ridx >= HALO + TOK_TILE) & (pos == tiles_per_seq - 1))
    n = jnp.where(dead, 0.0, _rms(x2, n2_ref[...])).astype(BF16)

    ffn = None
    for c in range(ffn_dim // FFN_CHUNK):
        halves = []
        for base in (0, ffn_dim):
            cols = slice(base + c * FFN_CHUNK, base + (c + 1) * FFN_CHUNK)
            h = jnp.dot(n, wu_ref[:, cols], preferred_element_type=F32)
            hc = (pltpu.roll(h, 1, 0) * cw_ref[0:1, cols] + h * cw_ref[1:2, cols]
                  + pltpu.roll(h, rows - 1, 0) * cw_ref[2:3, cols] + cb_ref[:, cols])
            halves.append(hc[HALO:HALO + TOK_TILE, :])
        gate, up = halves
        act = (gate * (1.0 / (1.0 + jnp.exp(-gate))) * up).astype(BF16)
        d = jnp.dot(act, wd_ref[c * FFN_CHUNK:(c + 1) * FFN_CHUNK, :], preferred_element_type=F32)
        ffn = d if ffn is None else ffn + d
    o_ref[...] = ffn + x2[HALO:HALO + TOK_TILE, :]


def _out_ffn(x2d, na, rt, seq_len, wo, n2, wu, cw, cb, wd):
    n_tok, d_model = x2d.shape
    ffn_dim = wd.shape[0]
    tiles_per_seq = seq_len // TOK_TILE
    per_tile = TOK_TILE // HALO
    last_halo = n_tok // HALO - 1
    cur = lambda i: (i, 0)
    prv = lambda i: (jnp.maximum(i * per_tile - 1, 0), 0)
    nxt = lambda i: (jnp.minimum((i + 1) * per_tile, last_halo), 0)
    main = lambda w: pl.BlockSpec((TOK_TILE, w), cur)
    hp = lambda w: pl.BlockSpec((HALO, w), prv)
    hn = lambda w: pl.BlockSpec((HALO, w), nxt)
    return pl.pallas_call(
        functools.partial(_out_ffn_kernel, tiles_per_seq=tiles_per_seq, ffn_dim=ffn_dim),
        out_shape=jax.ShapeDtypeStruct((n_tok, d_model), F32),
        grid=(n_tok // TOK_TILE,),
        in_specs=[main(d_model), hp(d_model), hn(d_model),
                  main(NA_WIDTH), hp(NA_WIDTH), hn(NA_WIDTH),
                  main(RET_WIDTH), hp(RET_WIDTH), hn(RET_WIDTH),
                  _const_spec(wo.shape), _const_spec(n2.shape), _const_spec(wu.shape),
                  _const_spec(cw.shape), _const_spec(cb.shape), _const_spec(wd.shape)],
        out_specs=main(d_model),
        compiler_params=pltpu.CompilerParams(dimension_semantics=("arbitrary",),
                                             vmem_limit_bytes=VMEM_LIMIT),
        name="out_ffn",
    )(x2d, x2d, x2d, na, na, na, rt, rt, rt, wo, n2, wu, cw, cb, wd)


def _rotary_tables(seq_len):
    half = RET_HEAD_DIM // 2
    inv_freq = ROPE_BASE ** (-jnp.arange(half, dtype=F32) / half)
    ang = jnp.arange(seq_len, dtype=F32)[:, None] * inv_freq[None, :]
    cos, sin = jnp.cos(ang), jnp.sin(ang)
    return jnp.concatenate([cos, cos], axis=-1), jnp.concatenate([-sin, sin], axis=-1)


def _layer(x, p):
    batch, seq_len, d_model = x.shape
    x2d = x.reshape(batch * seq_len, d_model)
    cs, sn = _rotary_tables(seq_len)
    naq, nak, nav, rq, rk, rv, rg = _in_proj(
        x2d, seq_len, p["n1"], p["w_in"], p["bd"], p["qn"], p["kn"], cs, sn)
    na = _na(naq, nak, nav, p["bias"], batch, seq_len)
    rt = _retention(rq, rk, rv, rg, p["lgf"], p["lgb"], p["nw"], batch, seq_len)
    y = _out_ffn(x2d, na, rt, seq_len, p["wo"], p["n2"], p["wu"], p["cw"], p["cb"], p["wd"])
    return y.reshape(batch, seq_len, d_model)


def _prepare(norm1_w, w_in, na_q_norm_w, na_k_norm_w, na_rpb, ret_theta_fwd, ret_theta_bwd,
             ret_norm_w, w_out, norm2_w, w_ffn_up, ffn_conv_w, ffn_conv_b, w_ffn_down):
    head = np.arange(NA_WIDTH) // NA_HEAD_DIM
    bd = jnp.asarray((head[:, None] == head[None, :]).astype(np.float32) / NA_HEAD_DIM, BF16)
    lanes = lambda t: jnp.repeat(t.astype(F32), RET_HEAD_DIM)[None, :]
    return dict(
        n1=norm1_w[None, :], w_in=w_in.astype(BF16), bd=bd,
        qn=jnp.tile(na_q_norm_w, NA_HEADS)[None, :], kn=jnp.tile(na_k_norm_w, NA_HEADS)[None, :],
        bias=_na_bias_tables(na_rpb),
        lgf=lanes(jax.nn.log_sigmoid(ret_theta_fwd.astype(F32))),
        lgb=lanes(jax.nn.log_sigmoid(ret_theta_bwd.astype(F32))),
        nw=ret_norm_w[None, :], wo=w_out.astype(BF16), n2=norm2_w[None, :],
        wu=w_ffn_up.astype(BF16), cw=ffn_conv_w, cb=ffn_conv_b[None, :], wd=w_ffn_down.astype(BF16))


def kernel(x_prompt, x_sample, norm1_w, w_in, na_q_norm_w, na_k_norm_w, na_rpb, ret_theta_fwd, ret_theta_bwd, ret_norm_w, w_out, norm2_w, w_ffn_up, ffn_conv_w, ffn_conv_b, w_ffn_down):
    depth = norm1_w.shape[0]
    y_prompt, y_sample = x_prompt, x_sample
    for l in range(depth):
        p = _prepare(norm1_w[l], w_in[l], na_q_norm_w[l], na_k_norm_w[l], na_rpb[l],
                     ret_theta_fwd[l], ret_theta_bwd[l], ret_norm_w[l], w_out[l], norm2_w[l],
                     w_ffn_up[l], ffn_conv_w[l], ffn_conv_b[l], w_ffn_down[l])
        y_prompt = _layer(y_prompt, p)
        y_sample = _layer(y_sample, p)
    return (y_prompt, y_sample)
```

```python
import functools

import jax
import jax.numpy as jnp
import numpy as np
from jax import lax
from jax.experimental import pallas as pl
from jax.experimental.pallas import tpu as pltpu

F32 = jnp.float32
BF16 = jnp.bfloat16

GRID_W = 64
NA_HEADS = 8
NA_HEAD_DIM = 64
NA_WIDTH = NA_HEADS * NA_HEAD_DIM
NA_WIN_R = 8
NA_WIN_C = 16
RET_HEADS = 4
RET_HEAD_DIM = 128
RET_WIDTH = RET_HEADS * RET_HEAD_DIM
RET_CHUNK = 128
ROPE_BASE = 10000.0
CONV_W = 3
NORM_EPS = 1e-6

LANES = 128
SUBLANES = 8
BF16_ROWS = 16
TOK_TILE = 512
NA_ROWS_PER_STEP = TOK_TILE // GRID_W
NA_GROUP_ROWS = 4
NA_KEY_ROWS = 12
NA_GROUP_TOK = NA_GROUP_ROWS * GRID_W
NA_KEY_TOK = NA_KEY_ROWS * GRID_W
MASK_VALUE = -1e30
FFN_CHUNK = 256
HALO = BF16_ROWS
VMEM_LIMIT = 56 * 1024 * 1024


def _const_spec(shape):
    return pl.BlockSpec(shape, lambda *_: (0,) * len(shape), pipeline_mode=pl.Buffered(1))


def _rms(x, w):
    ms = jnp.mean(x * x, axis=-1, keepdims=True)
    return x * lax.rsqrt(ms + NORM_EPS) * w


def _in_proj_kernel(x_ref, n1_ref, w_ref, bd_ref, qn_ref, kn_ref, cs_ref, sn_ref,
                    naq_ref, nak_ref, nav_ref, rq_ref, rk_ref, rv_ref, rg_ref):
    h = _rms(x_ref[...], n1_ref[...]).astype(BF16)

    def proj(idx):
        lo = idx * NA_WIDTH
        return jnp.dot(h, w_ref[:, lo:lo + NA_WIDTH], preferred_element_type=F32)

    def head_norm(t, w_row, scale):
        ms = jnp.dot((t * t).astype(BF16), bd_ref[...], preferred_element_type=F32)
        return t * lax.rsqrt(ms + NORM_EPS) * w_row * scale

    naq_ref[...] = head_norm(proj(0), qn_ref[...], NA_HEAD_DIM ** -0.5).astype(BF16)
    nak_ref[...] = head_norm(proj(1), kn_ref[...], 1.0).astype(BF16)
    nav_ref[...] = proj(2).astype(BF16)

    cs = cs_ref[...]
    sn = sn_ref[...]

    def rotary(t, scale):
        outs = []
        for hh in range(RET_HEADS):
            th = t[:, hh * LANES:(hh + 1) * LANES]
            r = th * cs + pltpu.roll(th, RET_HEAD_DIM // 2, 1) * sn
            outs.append(r * scale if scale != 1.0 else r)
        return jnp.concatenate(outs, axis=1)

    rq_ref[...] = rotary(proj(3), 1.0).astype(BF16)
    rk_ref[...] = rotary(proj(4), RET_HEAD_DIM ** -0.5).astype(BF16)
    rv_ref[...] = proj(5).astype(BF16)
    rg_ref[...] = proj(6).astype(BF16)


def _in_proj(x2d, seq_len, n1, w_in, bd, qn, kn, cs, sn):
    n_tok, d_model = x2d.shape
    tiles_per_seq = seq_len // TOK_TILE
    out = jax.ShapeDtypeStruct((n_tok, NA_WIDTH), BF16)
    tok_spec = lambda width: pl.BlockSpec((TOK_TILE, width), lambda i: (i, 0))
    pos_spec = pl.BlockSpec((TOK_TILE, LANES), lambda i: (i % tiles_per_seq, 0))
    return pl.pallas_call(
        _in_proj_kernel,
        out_shape=(out,) * 7,
        grid=(n_tok // TOK_TILE,),
        in_specs=[tok_spec(d_model), _const_spec(n1.shape), _const_spec(w_in.shape),
                  _const_spec(bd.shape), _const_spec(qn.shape), _const_spec(kn.shape),
                  pos_spec, pos_spec],
        out_specs=(tok_spec(NA_WIDTH),) * 7,
        compiler_params=pltpu.CompilerParams(dimension_semantics=("arbitrary",),
                                             vmem_limit_bytes=VMEM_LIMIT),
        name="in_proj",
    )(x2d, n1, w_in, bd, qn, kn, cs, sn)


def _na_kernel(q_ref, kp_ref, kc_ref, kn_ref, vp_ref, vc_ref, vn_ref, bias_ref, o_ref,
               kbuf, vbuf, *, grid_rows):
    r = pl.program_id(1)
    kbuf[0:TOK_TILE, :] = kp_ref[...]
    kbuf[TOK_TILE:2 * TOK_TILE, :] = kc_ref[...]
    kbuf[2 * TOK_TILE:3 * TOK_TILE, :] = kn_ref[...]
    vbuf[0:TOK_TILE, :] = vp_ref[...]
    vbuf[TOK_TILE:2 * TOK_TILE, :] = vc_ref[...]
    vbuf[2 * TOK_TILE:3 * TOK_TILE, :] = vn_ref[...]

    lane = lax.broadcasted_iota(jnp.int32, (NA_GROUP_TOK, LANES), 1)
    low_half = lane < NA_HEAD_DIM

    for g in range(NA_ROWS_PER_STEP // NA_GROUP_ROWS):
        row0 = r * NA_ROWS_PER_STEP + g * NA_GROUP_ROWS
        is_first = row0 == 0
        is_last = row0 == grid_rows - NA_GROUP_ROWS
        cls = jnp.where(is_first, 0, jnp.where(is_last, 2, 1))
        win_row = jnp.where(is_first, 0, jnp.where(is_last, grid_rows - NA_KEY_ROWS, row0 - NA_GROUP_ROWS))
        start = pl.multiple_of((win_row - (r - 1) * NA_ROWS_PER_STEP) * GRID_W, GRID_W)
        kwin = kbuf[pl.ds(start, NA_KEY_TOK), :]
        vwin = vbuf[pl.ds(start, NA_KEY_TOK), :]
        qg = q_ref[g * NA_GROUP_TOK:(g + 1) * NA_GROUP_TOK, :]
        for p in range(NA_HEADS // 2):
            sl = slice(p * LANES, (p + 1) * LANES)
            q2 = qg[:, sl]
            k2 = kwin[:, sl]
            v2 = vwin[:, sl]
            halves = []
            for hh in range(2):
                keep = low_half if hh == 0 else jnp.logical_not(low_half)
                qm = jnp.where(keep, q2, jnp.zeros_like(q2))
                s = lax.dot_general(qm, k2, (((1,), (1,)), ((), ())), preferred_element_type=F32)
                s = s + bias_ref[cls, 2 * p + hh]
                m = jnp.max(s, axis=-1, keepdims=True)
                e = jnp.exp(s - m)
                l = jnp.sum(e, axis=-1, keepdims=True)
                o2 = jnp.dot(e.astype(BF16), v2, preferred_element_type=F32)
                halves.append(o2 / l)
            o_ref[g * NA_GROUP_TOK:(g + 1) * NA_GROUP_TOK, sl] = (
                jnp.where(low_half, halves[0], halves[1]).astype(BF16))


def _na_bias_kernel(w_ref, o_ref):
    c = lax.broadcasted_iota(jnp.int32, (GRID_W, LANES), 0)
    lane = lax.broadcasted_iota(jnp.int32, (GRID_W, LANES), 1)
    kc = lane & (GRID_W - 1)
    col_start = jnp.clip(c - NA_WIN_C // 2, 0, GRID_W - NA_WIN_C)
    col_ok = (kc >= col_start) & (kc < col_start + NA_WIN_C)
    low = lane < GRID_W
    masked = jnp.full((GRID_W, LANES), MASK_VALUE, F32)

    def toeplitz(dr, lane_off):
        row = jnp.broadcast_to(w_ref[0, dr:dr + 1, :], (GRID_W, LANES))
        return pltpu.roll(row, (LANES - (NA_WIN_C - 1) + lane_off) % LANES, 1, stride=1, stride_axis=0)

    rel_offset = (0, -NA_GROUP_ROWS, -2 * NA_GROUP_ROWS)
    for cls in range(3):
        for a in range(NA_GROUP_ROWS):
            for bp in range(NA_KEY_ROWS // 2):
                halves = []
                for b in (2 * bp, 2 * bp + 1):
                    row_ok = (b < NA_WIN_R, 0 <= b - a < NA_WIN_R, b >= NA_KEY_ROWS - NA_WIN_R)[cls]
                    dr = b - a + rel_offset[cls] + NA_WIN_R - 1
                    halves.append(toeplitz(dr, (b % 2) * GRID_W) if row_ok else None)
                left, right = halves
                if left is None and right is None:
                    tile = masked
                else:
                    ok = col_ok
                    if left is None:
                        left, ok = masked, ok & jnp.logical_not(low)
                    if right is None:
                        right, ok = masked, ok & low
                    tile = jnp.where(ok, jnp.where(low, left, right), MASK_VALUE)
                o_ref[cls, 0, a * GRID_W:(a + 1) * GRID_W, bp * LANES:(bp + 1) * LANES] = tile


def _na_bias_tables(rpb):
    heads, n_dr, n_dc = rpb.shape
    w = jnp.pad(rpb.astype(F32), ((0, 0), (0, 0), (0, LANES - n_dc)))
    return pl.pallas_call(
        _na_bias_kernel,
        out_shape=jax.ShapeDtypeStruct((3, heads, NA_GROUP_TOK, NA_KEY_TOK), F32),
        grid=(heads,),
        in_specs=[pl.BlockSpec((1, n_dr, LANES), lambda h: (h, 0, 0))],
        out_specs=pl.BlockSpec((3, 1, NA_GROUP_TOK, NA_KEY_TOK), lambda h: (0, h, 0, 0)),
        compiler_params=pltpu.CompilerParams(dimension_semantics=("arbitrary",),
                                             vmem_limit_bytes=VMEM_LIMIT),
        name="na_bias",
    )(w)


def _na(q, k, v, bias, batch, seq_len):
    n_tok = q.shape[0]
    grid_rows = seq_len // GRID_W
    nb = seq_len // TOK_TILE
    cur = lambda b, r: (b * nb + r, 0)
    prv = lambda b, r: (b * nb + jnp.maximum(r - 1, 0), 0)
    nxt = lambda b, r: (b * nb + jnp.minimum(r + 1, nb - 1), 0)
    blk = lambda f: pl.BlockSpec((TOK_TILE, NA_WIDTH), f)
    return pl.pallas_call(
        functools.partial(_na_kernel, grid_rows=grid_rows),
        out_shape=jax.ShapeDtypeStruct((n_tok, NA_WIDTH), BF16),
        grid=(batch, nb),
        in_specs=[blk(cur), blk(prv), blk(cur), blk(nxt), blk(prv), blk(cur), blk(nxt),
                  _const_spec(bias.shape)],
        out_specs=blk(cur),
        scratch_shapes=[pltpu.VMEM((3 * TOK_TILE, NA_WIDTH), BF16),
                        pltpu.VMEM((3 * TOK_TILE, NA_WIDTH), BF16)],
        compiler_params=pltpu.CompilerParams(dimension_semantics=("arbitrary", "arbitrary"),
                                             vmem_limit_bytes=VMEM_LIMIT),
        name="na",
    )(q, k, k, k, v, v, v, bias)


def _ret_kernel(q_ref, k_ref, v_ref, g_ref, lgf_ref, lgb_ref, nw_ref, o_ref, t_all, state,
                *, steps):
    phase = pl.program_id(1)
    s = pl.program_id(2)
    chunks = TOK_TILE // RET_CHUNK
    C = RET_CHUNK

    @pl.when(s == 0)
    def _():
        state[...] = jnp.zeros_like(state)

    row = lax.broadcasted_iota(jnp.int32, (C, LANES), 0).astype(F32)
    col = lax.broadcasted_iota(jnp.int32, (C, LANES), 1).astype(F32)

    @pl.when(phase == 0)
    def _():
        tile = steps - 1 - s
        for c in reversed(range(chunks)):
            rows = slice(c * C, (c + 1) * C)
            n = tile * chunks + c
            for hh in range(RET_HEADS):
                sl = slice(hh * LANES, (hh + 1) * LANES)
                lgb = lgb_ref[:, sl]
                t_all[n, hh] = state[hh].astype(BF16)
                kd = (k_ref[rows, sl].astype(F32) * jnp.exp(lgb * row)).T.astype(BF16)
                kv = jnp.dot(kd, v_ref[rows, sl], preferred_element_type=F32)
                state[hh] = state[hh] * jnp.exp(lgb * C) + kv

    @pl.when(phase == 1)
    def _():
        for c in range(chunks):
            rows = slice(c * C, (c + 1) * C)
            n = s * chunks + c
            for hh in range(RET_HEADS):
                sl = slice(hh * LANES, (hh + 1) * LANES)
                lgf = lgf_ref[:, sl]
                lgb = lgb_ref[:, sl]
                q = q_ref[rows, sl]
                k = k_ref[rows, sl]
                v = v_ref[rows, sl]
                diff = row - col
                decay = jnp.where(diff >= 0, jnp.exp(lgf * jnp.maximum(diff, 0.0)),
                                  jnp.exp(lgb * jnp.maximum(-diff, 0.0)))
                a = lax.dot_general(q, k, (((1,), (1,)), ((), ())), preferred_element_type=F32)
                y = jnp.dot((a * decay).astype(BF16), v, preferred_element_type=F32)
                y = y + jnp.exp(lgf * (row + 1.0)) * jnp.dot(
                    q, state[hh].astype(BF16), preferred_element_type=F32)
                y = y + jnp.exp(lgb * (C - row)) * jnp.dot(
                    q, t_all[n, hh], preferred_element_type=F32)
                kd = (k.astype(F32) * jnp.exp(lgf * (C - 1.0 - row))).T.astype(BF16)
                kv = jnp.dot(kd, v, preferred_element_type=F32)
                state[hh] = state[hh] * jnp.exp(lgf * C) + kv
                g = g_ref[rows, sl].astype(F32)
                gate = g * (1.0 / (1.0 + jnp.exp(-g)))
                o_ref[rows, sl] = (gate * _rms(y, nw_ref[:, sl])).astype(BF16)


def _retention(q, k, v, g, lgf, lgb, nw, batch, seq_len):
    n_tok = q.shape[0]
    steps = seq_len // TOK_TILE
    n_chunks = seq_len // RET_CHUNK
    fwd_only = lambda b, p, s: (b * steps + s * p, 0)
    both = lambda b, p, s: (b * steps + jnp.where(p == 0, steps - 1 - s, s), 0)
    blk = lambda f: pl.BlockSpec((TOK_TILE, RET_WIDTH), f)
    return pl.pallas_call(
        functools.partial(_ret_kernel, steps=steps),
        out_shape=jax.ShapeDtypeStruct((n_tok, RET_WIDTH), BF16),
        grid=(batch, 2, steps),
        in_specs=[blk(fwd_only), blk(both), blk(both), blk(fwd_only),
                  _const_spec(lgf.shape), _const_spec(lgb.shape), _const_spec(nw.shape)],
        out_specs=blk(fwd_only),
        scratch_shapes=[pltpu.VMEM((n_chunks, RET_HEADS, RET_HEAD_DIM, RET_HEAD_DIM), BF16),
                        pltpu.VMEM((RET_HEADS, RET_HEAD_DIM, RET_HEAD_DIM), F32)],
        compiler_params=pltpu.CompilerParams(
            dimension_semantics=("arbitrary", "arbitrary", "arbitrary"),
            vmem_limit_bytes=VMEM_LIMIT),
        name="retention",
    )(q, k, v, g, lgf, lgb, nw)


def _out_ffn_kernel(x_ref, xp_ref, xn_ref, na_ref, nap_ref, nan_ref, rt_ref, rtp_ref, rtn_ref,
                    wo_ref, n2_ref, wu_ref, cw_ref, cb_ref, wd_ref, o_ref, n_scr, f_scr,
                    *, tiles_per_seq, ffn_dim):
    i = pl.program_id(0)
    rows = TOK_TILE + 2 * HALO
    slabs = n_scr.shape[0]
    x = jnp.concatenate([xp_ref[...], x_ref[...], xn_ref[...]], axis=0)
    na = jnp.concatenate([nap_ref[...], na_ref[...], nan_ref[...]], axis=0)
    rt = jnp.concatenate([rtp_ref[...], rt_ref[...], rtn_ref[...]], axis=0)
    x2 = (x + jnp.dot(na, wo_ref[:NA_WIDTH, :], preferred_element_type=F32)
          + jnp.dot(rt, wo_ref[NA_WIDTH:, :], preferred_element_type=F32))
    pos = i % tiles_per_seq
    ridx = lax.broadcasted_iota(jnp.int32, (rows, 1), 0)
    dead = ((ridx < HALO) & (pos == 0)) | ((ridx >= HALO + TOK_TILE) & (pos == tiles_per_seq - 1))
    nf = jnp.where(dead, 0.0, _rms(x2, n2_ref[...]))

    stride = rows // SUBLANES
    for j in range(slabs):
        n_scr[j] = nf[:, j * LANES:(j + 1) * LANES]
    n = jnp.concatenate(
        [jnp.concatenate([n_scr.at[j][pl.ds(v, SUBLANES, stride=stride), :] for j in range(slabs)], axis=1)
         for v in range(stride)], axis=0).astype(BF16)

    n_chunks = ffn_dim // FFN_CHUNK

    def up_proj(c):
        return [jnp.dot(n, wu_ref[:, base + c * FFN_CHUNK:base + (c + 1) * FFN_CHUNK],
                        preferred_element_type=F32) for base in (0, ffn_dim)]

    def conv(h, c, base):
        cols = slice(base + c * FFN_CHUNK, base + (c + 1) * FFN_CHUNK)
        prev = jnp.concatenate([pltpu.roll(h[rows - SUBLANES:], 1, 0), h[:rows - SUBLANES]], axis=0)
        nxt = jnp.concatenate([h[SUBLANES:], pltpu.roll(h[:SUBLANES], SUBLANES - 1, 0)], axis=0)
        return prev * cw_ref[0:1, cols] + h * cw_ref[1:2, cols] + nxt * cw_ref[2:3, cols] + cb_ref[:, cols]

    ffn = None
    h_next = up_proj(0)
    for c in range(n_chunks):
        h_gate, h_up = h_next
        if c + 1 < n_chunks:
            h_next = up_proj(c + 1)
        gate = conv(h_gate, c, 0)
        up = conv(h_up, c, ffn_dim)
        act = (gate * (1.0 / (1.0 + jnp.exp(-gate))) * up).astype(BF16)
        d = jnp.dot(act, wd_ref[c * FFN_CHUNK:(c + 1) * FFN_CHUNK, :], preferred_element_type=F32)
        ffn = d if ffn is None else ffn + d

    for v in range(stride):
        for j in range(slabs):
            f_scr.at[j][pl.ds(v, SUBLANES, stride=stride), :] = (
                ffn[v * SUBLANES:(v + 1) * SUBLANES, j * LANES:(j + 1) * LANES])
    for j in range(slabs):
        cols = slice(j * LANES, (j + 1) * LANES)
        o_ref[:, cols] = f_scr[j, HALO:HALO + TOK_TILE, :] + x2[HALO:HALO + TOK_TILE, cols]


def _out_ffn(x2d, na, rt, seq_len, wo, n2, wu, cw, cb, wd):
    n_tok, d_model = x2d.shape
    ffn_dim = wd.shape[0]
    tiles_per_seq = seq_len // TOK_TILE
    per_tile = TOK_TILE // HALO
    last_halo = n_tok // HALO - 1
    cur = lambda i: (i, 0)
    prv = lambda i: (jnp.maximum(i * per_tile - 1, 0), 0)
    nxt = lambda i: (jnp.minimum((i + 1) * per_tile, last_halo), 0)
    main = lambda w: pl.BlockSpec((TOK_TILE, w), cur)
    hp = lambda w: pl.BlockSpec((HALO, w), prv)
    hn = lambda w: pl.BlockSpec((HALO, w), nxt)
    return pl.pallas_call(
        functools.partial(_out_ffn_kernel, tiles_per_seq=tiles_per_seq, ffn_dim=ffn_dim),
        out_shape=jax.ShapeDtypeStruct((n_tok, d_model), F32),
        grid=(n_tok // TOK_TILE,),
        in_specs=[main(d_model), hp(d_model), hn(d_model),
                  main(NA_WIDTH), hp(NA_WIDTH), hn(NA_WIDTH),
                  main(RET_WIDTH), hp(RET_WIDTH), hn(RET_WIDTH),
                  _const_spec(wo.shape), _const_spec(n2.shape), _const_spec(wu.shape),
                  _const_spec(cw.shape), _const_spec(cb.shape), _const_spec(wd.shape)],
        out_specs=main(d_model),
        scratch_shapes=[pltpu.VMEM((d_model // LANES, TOK_TILE + 2 * HALO, LANES), F32),
                        pltpu.VMEM((d_model // LANES, TOK_TILE + 2 * HALO, LANES), F32)],
        compiler_params=pltpu.CompilerParams(dimension_semantics=("arbitrary",),
                                             vmem_limit_bytes=VMEM_LIMIT),
        name="out_ffn",
    )(x2d, x2d, x2d, na, na, na, rt, rt, rt, wo, n2, wu, cw, cb, wd)


def _rotary_tables(seq_len):
    half = RET_HEAD_DIM // 2
    inv_freq = ROPE_BASE ** (-jnp.arange(half, dtype=F32) / half)
    ang = jnp.arange(seq_len, dtype=F32)[:, None] * inv_freq[None, :]
    cos, sin = jnp.cos(ang), jnp.sin(ang)
    return jnp.concatenate([cos, cos], axis=-1), jnp.concatenate([-sin, sin], axis=-1)


def _layer(x, p, cs, sn):
    batch, seq_len, d_model = x.shape
    x2d = x.reshape(batch * seq_len, d_model)
    naq, nak, nav, rq, rk, rv, rg = _in_proj(
        x2d, seq_len, p["n1"], p["w_in"], p["bd"], p["qn"], p["kn"], cs, sn)
    na = _na(naq, nak, nav, p["bias"], batch, seq_len)
    rt = _retention(rq, rk, rv, rg, p["lgf"], p["lgb"], p["nw"], batch, seq_len)
    y = _out_ffn(x2d, na, rt, seq_len, p["wo"], p["n2"], p["wu"], p["cw"], p["cb"], p["wd"])
    return y.reshape(batch, seq_len, d_model)


def _prepare(norm1_w, w_in, na_q_norm_w, na_k_norm_w, na_rpb, ret_theta_fwd, ret_theta_bwd,
             ret_norm_w, w_out, norm2_w, w_ffn_up, ffn_conv_w, ffn_conv_b, w_ffn_down):
    head = np.arange(NA_WIDTH) // NA_HEAD_DIM
    bd = jnp.asarray((head[:, None] == head[None, :]).astype(np.float32) / NA_HEAD_DIM, BF16)
    lanes = lambda t: jnp.repeat(t.astype(F32), RET_HEAD_DIM)[None, :]
    return dict(
        n1=norm1_w[None, :], w_in=w_in.astype(BF16), bd=bd,
        qn=jnp.tile(na_q_norm_w, NA_HEADS)[None, :], kn=jnp.tile(na_k_norm_w, NA_HEADS)[None, :],
        bias=_na_bias_tables(na_rpb),
        lgf=lanes(jax.nn.log_sigmoid(ret_theta_fwd.astype(F32))),
        lgb=lanes(jax.nn.log_sigmoid(ret_theta_bwd.astype(F32))),
        nw=ret_norm_w[None, :], wo=w_out.astype(BF16), n2=norm2_w[None, :],
        wu=w_ffn_up.astype(BF16), cw=ffn_conv_w, cb=ffn_conv_b[None, :], wd=w_ffn_down.astype(BF16))


def kernel(x_prompt, x_sample, norm1_w, w_in, na_q_norm_w, na_k_norm_w, na_rpb, ret_theta_fwd, ret_theta_bwd, ret_norm_w, w_out, norm2_w, w_ffn_up, ffn_conv_w, ffn_conv_b, w_ffn_down):
    depth = norm1_w.shape[0]
    y_prompt, y_sample = x_prompt, x_sample
    cs, sn = _rotary_tables(max(x_prompt.shape[1], x_sample.shape[1]))
    for l in range(depth):
        p = _prepare(norm1_w[l], w_in[l], na_q_norm_w[l], na_k_norm_w[l], na_rpb[l],
                     ret_theta_fwd[l], ret_theta_bwd[l], ret_norm_w[l], w_out[l], norm2_w[l],
                     w_ffn_up[l], ffn_conv_w[l], ffn_conv_b[l], w_ffn_down[l])
        y_prompt = _layer(y_prompt, p, cs, sn)
        y_sample = _layer(y_sample, p, cs, sn)
    return (y_prompt, y_sample)
```

```python
import functools

import jax
import jax.numpy as jnp
import numpy as np
from jax import lax
from jax.experimental import pallas as pl
from jax.experimental.pallas import tpu as pltpu

F32 = jnp.float32
BF16 = jnp.bfloat16

GRID_W = 64
NA_HEADS = 8
NA_HEAD_DIM = 64
NA_WIDTH = NA_HEADS * NA_HEAD_DIM
NA_WIN_R = 8
NA_WIN_C = 16
RET_HEADS = 4
RET_HEAD_DIM = 128
RET_WIDTH = RET_HEADS * RET_HEAD_DIM
RET_CHUNK = 128
ROPE_BASE = 10000.0
CONV_W = 3
NORM_EPS = 1e-6

LANES = 128
SUBLANES = 8
BF16_ROWS = 16
IN_TILE = 1024
NA_TILE = 512
RET_TILE = 1024
TOK_TILE = 512
NA_ROWS_PER_STEP = NA_TILE // GRID_W
NA_GROUP_ROWS = 4
NA_KEY_ROWS = 12
NA_GROUP_TOK = NA_GROUP_ROWS * GRID_W
NA_KEY_TOK = NA_KEY_ROWS * GRID_W
MASK_VALUE = -1e30
FFN_CHUNK = 256
HALO = BF16_ROWS
VMEM_LIMIT = 56 * 1024 * 1024


def _const_spec(shape):
    return pl.BlockSpec(shape, lambda *_: (0,) * len(shape), pipeline_mode=pl.Buffered(1))


def _rms(x, w):
    ms = jnp.mean(x * x, axis=-1, keepdims=True)
    return x * lax.rsqrt(ms + NORM_EPS) * w


def _in_proj_kernel(x_ref, n1_ref, w_ref, bd_ref, qn_ref, kn_ref, cs_ref, sn_ref,
                    naq_ref, nak_ref, nav_ref, rq_ref, rk_ref, rv_ref, rg_ref):
    h = _rms(x_ref[...], n1_ref[...]).astype(BF16)

    def proj(idx):
        lo = idx * NA_WIDTH
        return jnp.dot(h, w_ref[:, lo:lo + NA_WIDTH], preferred_element_type=F32)

    def head_norm(t, w_row, scale):
        ms = jnp.dot((t * t).astype(BF16), bd_ref[...], preferred_element_type=F32)
        return t * lax.rsqrt(ms + NORM_EPS) * w_row * scale

    naq_ref[...] = head_norm(proj(0), qn_ref[...], NA_HEAD_DIM ** -0.5).astype(BF16)
    nak_ref[...] = head_norm(proj(1), kn_ref[...], 1.0).astype(BF16)
    nav_ref[...] = proj(2).astype(BF16)

    cs = cs_ref[...]
    sn = sn_ref[...]

    def rotary(t, scale):
        outs = []
        for hh in range(RET_HEADS):
            th = t[:, hh * LANES:(hh + 1) * LANES]
            r = th * cs + pltpu.roll(th, RET_HEAD_DIM // 2, 1) * sn
            outs.append(r * scale if scale != 1.0 else r)
        return jnp.concatenate(outs, axis=1)

    rq_ref[...] = rotary(proj(3), 1.0).astype(BF16)
    rk_ref[...] = rotary(proj(4), RET_HEAD_DIM ** -0.5).astype(BF16)
    rv_ref[...] = proj(5).astype(BF16)
    rg_ref[...] = proj(6).astype(BF16)


def _in_proj(x2d, seq_len, n1, w_in, bd, qn, kn, cs, sn):
    n_tok, d_model = x2d.shape
    tiles_per_seq = seq_len // IN_TILE
    out = jax.ShapeDtypeStruct((n_tok, NA_WIDTH), BF16)
    tok_spec = lambda width: pl.BlockSpec((IN_TILE, width), lambda i: (i, 0))
    pos_spec = pl.BlockSpec((IN_TILE, LANES), lambda i: (i % tiles_per_seq, 0))
    return pl.pallas_call(
        _in_proj_kernel,
        out_shape=(out,) * 7,
        grid=(n_tok // IN_TILE,),
        in_specs=[tok_spec(d_model), _const_spec(n1.shape), _const_spec(w_in.shape),
                  _const_spec(bd.shape), _const_spec(qn.shape), _const_spec(kn.shape),
                  pos_spec, pos_spec],
        out_specs=(tok_spec(NA_WIDTH),) * 7,
        compiler_params=pltpu.CompilerParams(dimension_semantics=("arbitrary",),
                                             vmem_limit_bytes=VMEM_LIMIT),
        name="in_proj",
    )(x2d, n1, w_in, bd, qn, kn, cs, sn)


def _na_kernel(q_ref, kp_ref, kc_ref, kn_ref, vp_ref, vc_ref, vn_ref, bias_ref, o_ref,
               kbuf, vbuf, *, grid_rows):
    r = pl.program_id(1)
    kbuf[0:NA_TILE, :] = kp_ref[...]
    kbuf[NA_TILE:2 * NA_TILE, :] = kc_ref[...]
    kbuf[2 * NA_TILE:3 * NA_TILE, :] = kn_ref[...]
    vbuf[0:NA_TILE, :] = vp_ref[...]
    vbuf[NA_TILE:2 * NA_TILE, :] = vc_ref[...]
    vbuf[2 * NA_TILE:3 * NA_TILE, :] = vn_ref[...]

    lane = lax.broadcasted_iota(jnp.int32, (NA_GROUP_TOK, LANES), 1)
    low_half = lane < NA_HEAD_DIM

    for g in range(NA_ROWS_PER_STEP // NA_GROUP_ROWS):
        row0 = r * NA_ROWS_PER_STEP + g * NA_GROUP_ROWS
        is_first = row0 == 0
        is_last = row0 == grid_rows - NA_GROUP_ROWS
        cls = jnp.where(is_first, 0, jnp.where(is_last, 2, 1))
        win_row = jnp.where(is_first, 0, jnp.where(is_last, grid_rows - NA_KEY_ROWS, row0 - NA_GROUP_ROWS))
        start = pl.multiple_of((win_row - (r - 1) * NA_ROWS_PER_STEP) * GRID_W, GRID_W)
        kwin = kbuf[pl.ds(start, NA_KEY_TOK), :]
        vwin = vbuf[pl.ds(start, NA_KEY_TOK), :]
        qg = q_ref[g * NA_GROUP_TOK:(g + 1) * NA_GROUP_TOK, :]
        for p in range(NA_HEADS // 2):
            sl = slice(p * LANES, (p + 1) * LANES)
            q2 = qg[:, sl]
            k2 = kwin[:, sl]
            v2 = vwin[:, sl]
            zero = jnp.zeros_like(q2)
            qs = jnp.concatenate([jnp.where(low_half, q2, zero), jnp.where(low_half, zero, q2)], axis=0)
            s = lax.dot_general(qs, k2, (((1,), (1,)), ((), ())), preferred_element_type=F32)
            s = s + bias_ref[cls, p]
            m = jnp.max(s, axis=-1, keepdims=True)
            e = jnp.exp(s - m)
            l = jnp.sum(e, axis=-1, keepdims=True)
            o2 = jnp.dot(e.astype(BF16), v2, preferred_element_type=F32) / l
            o_ref[g * NA_GROUP_TOK:(g + 1) * NA_GROUP_TOK, sl] = (
                jnp.where(low_half, o2[:NA_GROUP_TOK], o2[NA_GROUP_TOK:]).astype(BF16))


def _na_bias_kernel(w_ref, o_ref):
    c = lax.broadcasted_iota(jnp.int32, (GRID_W, LANES), 0)
    lane = lax.broadcasted_iota(jnp.int32, (GRID_W, LANES), 1)
    kc = lane & (GRID_W - 1)
    col_start = jnp.clip(c - NA_WIN_C // 2, 0, GRID_W - NA_WIN_C)
    col_ok = (kc >= col_start) & (kc < col_start + NA_WIN_C)
    low = lane < GRID_W
    masked = jnp.full((GRID_W, LANES), MASK_VALUE, F32)

    def toeplitz(dr, lane_off):
        row = jnp.broadcast_to(w_ref[0, dr:dr + 1, :], (GRID_W, LANES))
        return pltpu.roll(row, (LANES - (NA_WIN_C - 1) + lane_off) % LANES, 1, stride=1, stride_axis=0)

    rel_offset = (0, -NA_GROUP_ROWS, -2 * NA_GROUP_ROWS)
    for cls in range(3):
        for a in range(NA_GROUP_ROWS):
            for bp in range(NA_KEY_ROWS // 2):
                halves = []
                for b in (2 * bp, 2 * bp + 1):
                    row_ok = (b < NA_WIN_R, 0 <= b - a < NA_WIN_R, b >= NA_KEY_ROWS - NA_WIN_R)[cls]
                    dr = b - a + rel_offset[cls] + NA_WIN_R - 1
                    halves.append(toeplitz(dr, (b % 2) * GRID_W) if row_ok else None)
                left, right = halves
                if left is None and right is None:
                    tile = masked
                else:
                    ok = col_ok
                    if left is None:
                        left, ok = masked, ok & jnp.logical_not(low)
                    if right is None:
                        right, ok = masked, ok & low
                    tile = jnp.where(ok, jnp.where(low, left, right), MASK_VALUE)
                o_ref[cls, 0, a * GRID_W:(a + 1) * GRID_W, bp * LANES:(bp + 1) * LANES] = tile


def _na_bias_tables(rpb):
    heads, n_dr, n_dc = rpb.shape
    w = jnp.pad(rpb.astype(F32), ((0, 0), (0, 0), (0, LANES - n_dc)))
    return pl.pallas_call(
        _na_bias_kernel,
        out_shape=jax.ShapeDtypeStruct((3, heads, NA_GROUP_TOK, NA_KEY_TOK), F32),
        grid=(heads,),
        in_specs=[pl.BlockSpec((1, n_dr, LANES), lambda h: (h, 0, 0))],
        out_specs=pl.BlockSpec((3, 1, NA_GROUP_TOK, NA_KEY_TOK), lambda h: (0, h, 0, 0)),
        compiler_params=pltpu.CompilerParams(dimension_semantics=("arbitrary",),
                                             vmem_limit_bytes=VMEM_LIMIT),
        name="na_bias",
    )(w)


def _na(q, k, v, bias, batch, seq_len):
    n_tok = q.shape[0]
    grid_rows = seq_len // GRID_W
    nb = seq_len // NA_TILE
    cur = lambda b, r: (b * nb + r, 0)
    prv = lambda b, r: (b * nb + jnp.maximum(r - 1, 0), 0)
    nxt = lambda b, r: (b * nb + jnp.minimum(r + 1, nb - 1), 0)
    blk = lambda f: pl.BlockSpec((NA_TILE, NA_WIDTH), f)
    return pl.pallas_call(
        functools.partial(_na_kernel, grid_rows=grid_rows),
        out_shape=jax.ShapeDtypeStruct((n_tok, NA_WIDTH), BF16),
        grid=(batch, nb),
        in_specs=[blk(cur), blk(prv), blk(cur), blk(nxt), blk(prv), blk(cur), blk(nxt),
                  _const_spec(bias.shape)],
        out_specs=blk(cur),
        scratch_shapes=[pltpu.VMEM((3 * NA_TILE, NA_WIDTH), BF16),
                        pltpu.VMEM((3 * NA_TILE, NA_WIDTH), BF16)],
        compiler_params=pltpu.CompilerParams(dimension_semantics=("arbitrary", "arbitrary"),
                                             vmem_limit_bytes=VMEM_LIMIT),
        name="na",
    )(q, k, k, k, v, v, v, bias)


def _ret_kernel(q_ref, k_ref, v_ref, g_ref, lgf_ref, lgb_ref, nw_ref, o_ref, t_all, state,
                *, steps):
    phase = pl.program_id(1)
    s = pl.program_id(2)
    chunks = RET_TILE // RET_CHUNK
    C = RET_CHUNK

    @pl.when(s == 0)
    def _():
        state[...] = jnp.zeros_like(state)

    row = lax.broadcasted_iota(jnp.int32, (C, LANES), 0).astype(F32)
    col = lax.broadcasted_iota(jnp.int32, (C, LANES), 1).astype(F32)

    @pl.when(phase == 0)
    def _():
        tile = steps - 1 - s
        heads = [slice(hh * LANES, (hh + 1) * LANES) for hh in range(RET_HEADS)]
        k_decay = [jnp.exp(lgb_ref[:, sl] * row) for sl in heads]
        chunk_decay = [jnp.exp(lgb_ref[:, sl] * C) for sl in heads]
        for c in reversed(range(chunks)):
            rows = slice(c * C, (c + 1) * C)
            for hh, sl in enumerate(heads):
                t_all[tile * chunks + c, hh] = state[hh].astype(BF16)
                kd = (k_ref[rows, sl].astype(F32) * k_decay[hh]).T.astype(BF16)
                kv = jnp.dot(kd, v_ref[rows, sl], preferred_element_type=F32)
                state[hh] = state[hh] * chunk_decay[hh] + kv

    @pl.when(phase == 1)
    def _():
        diff = row - col
        heads = [slice(hh * LANES, (hh + 1) * LANES) for hh in range(RET_HEADS)]
        decay = [jnp.where(diff >= 0, jnp.exp(lgf_ref[:, sl] * jnp.maximum(diff, 0.0)),
                           jnp.exp(lgb_ref[:, sl] * jnp.maximum(-diff, 0.0))) for sl in heads]
        q_decay_f = [jnp.exp(lgf_ref[:, sl] * (row + 1.0)) for sl in heads]
        q_decay_b = [jnp.exp(lgb_ref[:, sl] * (C - row)) for sl in heads]
        k_decay = [jnp.exp(lgf_ref[:, sl] * (C - 1.0 - row)) for sl in heads]
        chunk_decay = [jnp.exp(lgf_ref[:, sl] * C) for sl in heads]
        for c in range(chunks):
            rows = slice(c * C, (c + 1) * C)
            for hh, sl in enumerate(heads):
                q = q_ref[rows, sl]
                k = k_ref[rows, sl]
                v = v_ref[rows, sl]
                a = lax.dot_general(q, k, (((1,), (1,)), ((), ())), preferred_element_type=F32)
                y = jnp.dot((a * decay[hh]).astype(BF16), v, preferred_element_type=F32)
                y = y + q_decay_f[hh] * jnp.dot(q, state[hh].astype(BF16), preferred_element_type=F32)
                y = y + q_decay_b[hh] * jnp.dot(q, t_all[s * chunks + c, hh],
                                                preferred_element_type=F32)
                kd = (k.astype(F32) * k_decay[hh]).T.astype(BF16)
                kv = jnp.dot(kd, v, preferred_element_type=F32)
                state[hh] = state[hh] * chunk_decay[hh] + kv
                g = g_ref[rows, sl].astype(F32)
                gate = g * (1.0 / (1.0 + jnp.exp(-g)))
                o_ref[rows, sl] = (gate * _rms(y, nw_ref[:, sl])).astype(BF16)


def _retention(q, k, v, g, lgf, lgb, nw, batch, seq_len):
    n_tok = q.shape[0]
    steps = seq_len // RET_TILE
    n_chunks = seq_len // RET_CHUNK
    fwd_only = lambda b, p, s: (b * steps + s * p, 0)
    both = lambda b, p, s: (b * steps + jnp.where(p == 0, steps - 1 - s, s), 0)
    blk = lambda f: pl.BlockSpec((RET_TILE, RET_WIDTH), f)
    return pl.pallas_call(
        functools.partial(_ret_kernel, steps=steps),
        out_shape=jax.ShapeDtypeStruct((n_tok, RET_WIDTH), BF16),
        grid=(batch, 2, steps),
        in_specs=[blk(fwd_only), blk(both), blk(both), blk(fwd_only),
                  _const_spec(lgf.shape), _const_spec(lgb.shape), _const_spec(nw.shape)],
        out_specs=blk(fwd_only),
        scratch_shapes=[pltpu.VMEM((n_chunks, RET_HEADS, RET_HEAD_DIM, RET_HEAD_DIM), BF16),
                        pltpu.VMEM((RET_HEADS, RET_HEAD_DIM, RET_HEAD_DIM), F32)],
        compiler_params=pltpu.CompilerParams(
            dimension_semantics=("arbitrary", "arbitrary", "arbitrary"),
            vmem_limit_bytes=VMEM_LIMIT),
        name="retention",
    )(q, k, v, g, lgf, lgb, nw)


def _out_ffn_kernel(x_ref, xp_ref, xn_ref, na_ref, nap_ref, nan_ref, rt_ref, rtp_ref, rtn_ref,
                    wo_ref, n2_ref, wu_ref, cw_ref, cb_ref, wd_ref, o_ref, n_scr, f_scr, act_scr,
                    *, tiles_per_seq, ffn_dim):
    i = pl.program_id(0)
    rows = TOK_TILE + 2 * HALO
    slabs = n_scr.shape[0]
    x = jnp.concatenate([xp_ref[...], x_ref[...], xn_ref[...]], axis=0)
    na = jnp.concatenate([nap_ref[...], na_ref[...], nan_ref[...]], axis=0)
    rt = jnp.concatenate([rtp_ref[...], rt_ref[...], rtn_ref[...]], axis=0)
    x2 = (x + jnp.dot(na, wo_ref[:NA_WIDTH, :], preferred_element_type=F32)
          + jnp.dot(rt, wo_ref[NA_WIDTH:, :], preferred_element_type=F32))
    pos = i % tiles_per_seq
    ridx = lax.broadcasted_iota(jnp.int32, (rows, 1), 0)
    dead = ((ridx < HALO) & (pos == 0)) | ((ridx >= HALO + TOK_TILE) & (pos == tiles_per_seq - 1))
    nf = jnp.where(dead, 0.0, _rms(x2, n2_ref[...]))

    stride = rows // SUBLANES
    for j in range(slabs):
        n_scr[j] = nf[:, j * LANES:(j + 1) * LANES]
    n = jnp.concatenate(
        [jnp.concatenate([n_scr.at[j][pl.ds(v, SUBLANES, stride=stride), :] for j in range(slabs)], axis=1)
         for v in range(stride)], axis=0).astype(BF16)

    n_chunks = ffn_dim // FFN_CHUNK

    def up_proj(c):
        return [jnp.dot(n, wu_ref[:, base + c * FFN_CHUNK:base + (c + 1) * FFN_CHUNK],
                        preferred_element_type=F32) for base in (0, ffn_dim)]

    def conv(h, c, base):
        cols = slice(base + c * FFN_CHUNK, base + (c + 1) * FFN_CHUNK)
        prev = jnp.concatenate([pltpu.roll(h[rows - SUBLANES:], 1, 0), h[:rows - SUBLANES]], axis=0)
        nxt = jnp.concatenate([h[SUBLANES:], pltpu.roll(h[:SUBLANES], SUBLANES - 1, 0)], axis=0)
        return prev * cw_ref[0:1, cols] + h * cw_ref[1:2, cols] + nxt * cw_ref[2:3, cols] + cb_ref[:, cols]

    for c in range(n_chunks):
        h_gate, h_up = up_proj(c)
        gate = conv(h_gate, c, 0)
        up = conv(h_up, c, ffn_dim)
        act_scr[:, c * FFN_CHUNK:(c + 1) * FFN_CHUNK] = (
            gate * (1.0 / (1.0 + jnp.exp(-gate))) * up).astype(BF16)
    ffn = jnp.dot(act_scr[...], wd_ref[...], preferred_element_type=F32)

    for v in range(stride):
        for j in range(slabs):
            f_scr.at[j][pl.ds(v, SUBLANES, stride=stride), :] = (
                ffn[v * SUBLANES:(v + 1) * SUBLANES, j * LANES:(j + 1) * LANES])
    for j in range(slabs):
        cols = slice(j * LANES, (j + 1) * LANES)
        o_ref[:, cols] = f_scr[j, HALO:HALO + TOK_TILE, :] + x2[HALO:HALO + TOK_TILE, cols]


def _out_ffn(x2d, na, rt, seq_len, wo, n2, wu, cw, cb, wd):
    n_tok, d_model = x2d.shape
    ffn_dim = wd.shape[0]
    tiles_per_seq = seq_len // TOK_TILE
    per_tile = TOK_TILE // HALO
    last_halo = n_tok // HALO - 1
    cur = lambda i: (i, 0)
    prv = lambda i: (jnp.maximum(i * per_tile - 1, 0), 0)
    nxt = lambda i: (jnp.minimum((i + 1) * per_tile, last_halo), 0)
    main = lambda w: pl.BlockSpec((TOK_TILE, w), cur)
    hp = lambda w: pl.BlockSpec((HALO, w), prv)
    hn = lambda w: pl.BlockSpec((HALO, w), nxt)
    return pl.pallas_call(
        functools.partial(_out_ffn_kernel, tiles_per_seq=tiles_per_seq, ffn_dim=ffn_dim),
        out_shape=jax.ShapeDtypeStruct((n_tok, d_model), F32),
        grid=(n_tok // TOK_TILE,),
        in_specs=[main(d_model), hp(d_model), hn(d_model),
                  main(NA_WIDTH), hp(NA_WIDTH), hn(NA_WIDTH),
                  main(RET_WIDTH), hp(RET_WIDTH), hn(RET_WIDTH),
                  _const_spec(wo.shape), _const_spec(n2.shape), _const_spec(wu.shape),
                  _const_spec(cw.shape), _const_spec(cb.shape), _const_spec(wd.shape)],
        out_specs=main(d_model),
        scratch_shapes=[pltpu.VMEM((d_model // LANES, TOK_TILE + 2 * HALO, LANES), F32),
                        pltpu.VMEM((d_model // LANES, TOK_TILE + 2 * HALO, LANES), F32),
                        pltpu.VMEM((TOK_TILE + 2 * HALO, ffn_dim), BF16)],
        compiler_params=pltpu.CompilerParams(dimension_semantics=("arbitrary",),
                                             vmem_limit_bytes=VMEM_LIMIT),
        name="out_ffn",
    )(x2d, x2d, x2d, na, na, na, rt, rt, rt, wo, n2, wu, cw, cb, wd)


def _rotary_tables(seq_len):
    half = RET_HEAD_DIM // 2
    inv_freq = ROPE_BASE ** (-jnp.arange(half, dtype=F32) / half)
    ang = jnp.arange(seq_len, dtype=F32)[:, None] * inv_freq[None, :]
    cos, sin = jnp.cos(ang), jnp.sin(ang)
    return jnp.concatenate([cos, cos], axis=-1), jnp.concatenate([-sin, sin], axis=-1)


def _layer(x, p, cs, sn):
    batch, seq_len, d_model = x.shape
    x2d = x.reshape(batch * seq_len, d_model)
    naq, nak, nav, rq, rk, rv, rg = _in_proj(
        x2d, seq_len, p["n1"], p["w_in"], p["bd"], p["qn"], p["kn"], cs, sn)
    na = _na(naq, nak, nav, p["bias"], batch, seq_len)
    rt = _retention(rq, rk, rv, rg, p["lgf"], p["lgb"], p["nw"], batch, seq_len)
    y = _out_ffn(x2d, na, rt, seq_len, p["wo"], p["n2"], p["wu"], p["cw"], p["cb"], p["wd"])
    return y.reshape(batch, seq_len, d_model)


def _prepare(norm1_w, w_in, na_q_norm_w, na_k_norm_w, na_rpb, ret_theta_fwd, ret_theta_bwd,
             ret_norm_w, w_out, norm2_w, w_ffn_up, ffn_conv_w, ffn_conv_b, w_ffn_down):
    head = np.arange(NA_WIDTH) // NA_HEAD_DIM
    bd = jnp.asarray((head[:, None] == head[None, :]).astype(np.float32) / NA_HEAD_DIM, BF16)
    lanes = lambda t: jnp.repeat(t.astype(F32), RET_HEAD_DIM)[None, :]
    return dict(
        n1=norm1_w[None, :], w_in=w_in.astype(BF16), bd=bd,
        qn=jnp.tile(na_q_norm_w, NA_HEADS)[None, :], kn=jnp.tile(na_k_norm_w, NA_HEADS)[None, :],
        bias=_na_bias_tables(na_rpb).reshape(3, NA_HEADS // 2, 2 * NA_GROUP_TOK, NA_KEY_TOK),
        lgf=lanes(jax.nn.log_sigmoid(ret_theta_fwd.astype(F32))),
        lgb=lanes(jax.nn.log_sigmoid(ret_theta_bwd.astype(F32))),
        nw=ret_norm_w[None, :], wo=w_out.astype(BF16), n2=norm2_w[None, :],
        wu=w_ffn_up.astype(BF16), cw=ffn_conv_w, cb=ffn_conv_b[None, :], wd=w_ffn_down.astype(BF16))


def kernel(x_prompt, x_sample, norm1_w, w_in, na_q_norm_w, na_k_norm_w, na_rpb, ret_theta_fwd, ret_theta_bwd, ret_norm_w, w_out, norm2_w, w_ffn_up, ffn_conv_w, ffn_conv_b, w_ffn_down):
    depth = norm1_w.shape[0]
    y_prompt, y_sample = x_prompt, x_sample
    cs, sn = _rotary_tables(max(x_prompt.shape[1], x_sample.shape[1]))
    for l in range(depth):
        p = _prepare(norm1_w[l], w_in[l], na_q_norm_w[l], na_k_norm_w[l], na_rpb[l],
                     ret_theta_fwd[l], ret_theta_bwd[l], ret_norm_w[l], w_out[l], norm2_w[l],
                     w_ffn_up[l], ffn_conv_w[l], ffn_conv_b[l], w_ffn_down[l])
        y_prompt = _layer(y_prompt, p, cs, sn)
        y_sample = _layer(y_sample, p, cs, sn)
    return (y_prompt, y_sample)
```

```python
import functools

import jax
import jax.numpy as jnp
import numpy as np
from jax import lax
from jax.experimental import pallas as pl
from jax.experimental.pallas import tpu as pltpu

F32 = jnp.float32
BF16 = jnp.bfloat16

GRID_W = 64
NA_HEADS = 8
NA_HEAD_DIM = 64
NA_WIDTH = NA_HEADS * NA_HEAD_DIM
NA_WIN_R = 8
NA_WIN_C = 16
RET_HEADS = 4
RET_HEAD_DIM = 128
RET_WIDTH = RET_HEADS * RET_HEAD_DIM
RET_CHUNK = 128
ROPE_BASE = 10000.0
CONV_W = 3
NORM_EPS = 1e-6

LANES = 128
SUBLANES = 8
BF16_ROWS = 16
IN_TILE = 1024
NA_TILE = 1024
RET_TILE = 1024
TOK_TILE = 512
NA_ROWS_PER_STEP = NA_TILE // GRID_W
NA_GROUP_ROWS = 4
NA_KEY_ROWS = 12
NA_GROUP_TOK = NA_GROUP_ROWS * GRID_W
NA_KEY_TOK = NA_KEY_ROWS * GRID_W
MASK_VALUE = -1e30
FFN_CHUNK = 256
HALO = BF16_ROWS
VMEM_LIMIT = 56 * 1024 * 1024


def _const_spec(shape):
    return pl.BlockSpec(shape, lambda *_: (0,) * len(shape), pipeline_mode=pl.Buffered(1))


def _rms(x, w):
    ms = jnp.mean(x * x, axis=-1, keepdims=True)
    return x * lax.rsqrt(ms + NORM_EPS) * w


def _in_proj_kernel(x_ref, n1_ref, w_ref, bd_ref, qn_ref, kn_ref, freq_ref, cosr_ref, sinr_ref,
                    naq_ref, nak_ref, nav_ref, rq_ref, rk_ref, rv_ref, rg_ref, *, tiles_per_seq):
    h = _rms(x_ref[...], n1_ref[...]).astype(BF16)

    def proj(idx):
        lo = idx * NA_WIDTH
        return jnp.dot(h, w_ref[:, lo:lo + NA_WIDTH], preferred_element_type=F32)

    def head_norm(t, w_row, scale):
        ms = jnp.dot((t * t).astype(BF16), bd_ref[...], preferred_element_type=F32)
        return t * lax.rsqrt(ms + NORM_EPS) * w_row * scale

    naq_ref[...] = head_norm(proj(0), qn_ref[...], NA_HEAD_DIM ** -0.5).astype(BF16)
    nak_ref[...] = head_norm(proj(1), kn_ref[...], 1.0).astype(BF16)
    nav_ref[...] = proj(2).astype(BF16)

    first = ((pl.program_id(0) % tiles_per_seq) * IN_TILE).astype(F32)
    a0 = jnp.broadcast_to(first * freq_ref[...], (SUBLANES, LANES))
    c0 = jnp.cos(a0)[0:1, :]
    s0 = jnp.sin(a0)[0:1, :]
    cos_r = cosr_ref[...]
    sin_r = sinr_ref[...]
    lower = lax.broadcasted_iota(jnp.int32, (IN_TILE, LANES), 1) < RET_HEAD_DIM // 2
    cs = c0 * cos_r - s0 * sin_r
    sin = s0 * cos_r + c0 * sin_r
    sn = jnp.where(lower, -sin, sin)

    def rotary(t, scale):
        outs = []
        for hh in range(RET_HEADS):
            th = t[:, hh * LANES:(hh + 1) * LANES]
            r = th * cs + pltpu.roll(th, RET_HEAD_DIM // 2, 1) * sn
            outs.append(r * scale if scale != 1.0 else r)
        return jnp.concatenate(outs, axis=1)

    rq_ref[...] = rotary(proj(3), 1.0).astype(BF16)
    rk_ref[...] = rotary(proj(4), RET_HEAD_DIM ** -0.5).astype(BF16)
    rv_ref[...] = proj(5).astype(BF16)
    rg_ref[...] = proj(6).astype(BF16)


def _in_proj(x2d, seq_len, n1, w_in, bd, qn, kn, rope):
    freq, cos_r, sin_r = rope
    n_tok, d_model = x2d.shape
    out = jax.ShapeDtypeStruct((n_tok, NA_WIDTH), BF16)
    tok_spec = lambda width: pl.BlockSpec((IN_TILE, width), lambda i: (i, 0))
    return pl.pallas_call(
        functools.partial(_in_proj_kernel, tiles_per_seq=seq_len // IN_TILE),
        out_shape=(out,) * 7,
        grid=(n_tok // IN_TILE,),
        in_specs=[tok_spec(d_model), _const_spec(n1.shape), _const_spec(w_in.shape),
                  _const_spec(bd.shape), _const_spec(qn.shape), _const_spec(kn.shape),
                  _const_spec(freq.shape), _const_spec(cos_r.shape), _const_spec(sin_r.shape)],
        out_specs=(tok_spec(NA_WIDTH),) * 7,
        compiler_params=pltpu.CompilerParams(dimension_semantics=("arbitrary",),
                                             vmem_limit_bytes=VMEM_LIMIT),
        name="in_proj",
    )(x2d, n1, w_in, bd, qn, kn, freq, cos_r, sin_r)


def _na_kernel(q_ref, kp_ref, kc_ref, kn_ref, vp_ref, vc_ref, vn_ref, bias_ref, o_ref,
               kbuf, vbuf, *, grid_rows):
    r = pl.program_id(1)
    kbuf[0:NA_GROUP_TOK, :] = kp_ref[...]
    kbuf[NA_GROUP_TOK:NA_GROUP_TOK + NA_TILE, :] = kc_ref[...]
    kbuf[NA_GROUP_TOK + NA_TILE:, :] = kn_ref[...]
    vbuf[0:NA_GROUP_TOK, :] = vp_ref[...]
    vbuf[NA_GROUP_TOK:NA_GROUP_TOK + NA_TILE, :] = vc_ref[...]
    vbuf[NA_GROUP_TOK + NA_TILE:, :] = vn_ref[...]

    lane = lax.broadcasted_iota(jnp.int32, (NA_GROUP_TOK, LANES), 1)
    low_half = lane < NA_HEAD_DIM

    for g in range(NA_ROWS_PER_STEP // NA_GROUP_ROWS):
        row0 = r * NA_ROWS_PER_STEP + g * NA_GROUP_ROWS
        is_first = row0 == 0
        is_last = row0 == grid_rows - NA_GROUP_ROWS
        cls = jnp.where(is_first, 0, jnp.where(is_last, 2, 1))
        win_row = jnp.where(is_first, 0, jnp.where(is_last, grid_rows - NA_KEY_ROWS, row0 - NA_GROUP_ROWS))
        start = pl.multiple_of(
            (win_row - (r * NA_ROWS_PER_STEP - NA_GROUP_ROWS)) * GRID_W, NA_GROUP_TOK)
        kwin = kbuf[pl.ds(start, NA_KEY_TOK), :]
        vwin = vbuf[pl.ds(start, NA_KEY_TOK), :]
        qg = q_ref[g * NA_GROUP_TOK:(g + 1) * NA_GROUP_TOK, :]
        for p in range(NA_HEADS // 2):
            sl = slice(p * LANES, (p + 1) * LANES)
            q2 = qg[:, sl]
            k2 = kwin[:, sl]
            v2 = vwin[:, sl]
            zero = jnp.zeros_like(q2)
            qs = jnp.concatenate([jnp.where(low_half, q2, zero), jnp.where(low_half, zero, q2)], axis=0)
            s = lax.dot_general(qs, k2, (((1,), (1,)), ((), ())), preferred_element_type=F32)
            s = s + bias_ref[cls, p]
            m = jnp.max(s, axis=-1, keepdims=True)
            e = jnp.exp(s - m)
            l = jnp.sum(e, axis=-1, keepdims=True)
            o2 = jnp.dot(e.astype(BF16), v2, preferred_element_type=F32) / l
            o_ref[g * NA_GROUP_TOK:(g + 1) * NA_GROUP_TOK, sl] = (
                jnp.where(low_half, o2[:NA_GROUP_TOK], o2[NA_GROUP_TOK:]).astype(BF16))


def _na_bias_kernel(w_ref, o_ref):
    c = lax.broadcasted_iota(jnp.int32, (GRID_W, LANES), 0)
    lane = lax.broadcasted_iota(jnp.int32, (GRID_W, LANES), 1)
    kc = lane & (GRID_W - 1)
    col_start = jnp.clip(c - NA_WIN_C // 2, 0, GRID_W - NA_WIN_C)
    col_ok = (kc >= col_start) & (kc < col_start + NA_WIN_C)
    low = lane < GRID_W
    masked = jnp.full((GRID_W, LANES), MASK_VALUE, F32)

    def toeplitz(dr, lane_off):
        row = jnp.broadcast_to(w_ref[0, dr:dr + 1, :], (GRID_W, LANES))
        return pltpu.roll(row, (LANES - (NA_WIN_C - 1) + lane_off) % LANES, 1, stride=1, stride_axis=0)

    rel_offset = (0, -NA_GROUP_ROWS, -2 * NA_GROUP_ROWS)
    for cls in range(3):
        for a in range(NA_GROUP_ROWS):
            for bp in range(NA_KEY_ROWS // 2):
                halves = []
                for b in (2 * bp, 2 * bp + 1):
                    row_ok = (b < NA_WIN_R, 0 <= b - a < NA_WIN_R, b >= NA_KEY_ROWS - NA_WIN_R)[cls]
                    dr = b - a + rel_offset[cls] + NA_WIN_R - 1
                    halves.append(toeplitz(dr, (b % 2) * GRID_W) if row_ok else None)
                left, right = halves
                if left is None and right is None:
                    tile = masked
                else:
                    ok = col_ok
                    if left is None:
                        left, ok = masked, ok & jnp.logical_not(low)
                    if right is None:
                        right, ok = masked, ok & low
                    tile = jnp.where(ok, jnp.where(low, left, right), MASK_VALUE)
                o_ref[cls, 0, a * GRID_W:(a + 1) * GRID_W, bp * LANES:(bp + 1) * LANES] = tile


def _na_bias_tables(rpb):
    heads, n_dr, n_dc = rpb.shape
    w = jnp.pad(rpb.astype(F32), ((0, 0), (0, 0), (0, LANES - n_dc)))
    return pl.pallas_call(
        _na_bias_kernel,
        out_shape=jax.ShapeDtypeStruct((3, heads, NA_GROUP_TOK, NA_KEY_TOK), F32),
        grid=(heads,),
        in_specs=[pl.BlockSpec((1, n_dr, LANES), lambda h: (h, 0, 0))],
        out_specs=pl.BlockSpec((3, 1, NA_GROUP_TOK, NA_KEY_TOK), lambda h: (0, h, 0, 0)),
        compiler_params=pltpu.CompilerParams(dimension_semantics=("arbitrary",),
                                             vmem_limit_bytes=VMEM_LIMIT),
        name="na_bias",
    )(w)


def _na(q, k, v, bias, batch, seq_len):
    n_tok = q.shape[0]
    grid_rows = seq_len // GRID_W
    nb = seq_len // NA_TILE
    per = NA_TILE // NA_GROUP_TOK
    nh = seq_len // NA_GROUP_TOK
    cur = lambda b, r: (b * nb + r, 0)
    prv = lambda b, r: (b * nh + jnp.maximum(r * per - 1, 0), 0)
    nxt = lambda b, r: (b * nh + jnp.minimum((r + 1) * per, nh - 1), 0)
    blk = lambda f: pl.BlockSpec((NA_TILE, NA_WIDTH), f)
    halo = lambda f: pl.BlockSpec((NA_GROUP_TOK, NA_WIDTH), f)
    return pl.pallas_call(
        functools.partial(_na_kernel, grid_rows=grid_rows),
        out_shape=jax.ShapeDtypeStruct((n_tok, NA_WIDTH), BF16),
        grid=(batch, nb),
        in_specs=[blk(cur), halo(prv), blk(cur), halo(nxt), halo(prv), blk(cur), halo(nxt),
                  _const_spec(bias.shape)],
        out_specs=blk(cur),
        scratch_shapes=[pltpu.VMEM((NA_TILE + 2 * NA_GROUP_TOK, NA_WIDTH), BF16),
                        pltpu.VMEM((NA_TILE + 2 * NA_GROUP_TOK, NA_WIDTH), BF16)],
        compiler_params=pltpu.CompilerParams(dimension_semantics=("arbitrary", "arbitrary"),
                                             vmem_limit_bytes=VMEM_LIMIT),
        name="na",
    )(q, k, k, k, v, v, v, bias)


def _ret_kernel(q_ref, k_ref, v_ref, g_ref, lgf_ref, lgb_ref, nw_ref, o_ref, t_all, state,
                *, steps):
    phase = pl.program_id(1)
    s = pl.program_id(2)
    chunks = RET_TILE // RET_CHUNK
    C = RET_CHUNK

    @pl.when(s == 0)
    def _():
        state[...] = jnp.zeros_like(state)

    row = lax.broadcasted_iota(jnp.int32, (C, LANES), 0).astype(F32)
    col = lax.broadcasted_iota(jnp.int32, (C, LANES), 1).astype(F32)

    @pl.when(phase == 0)
    def _():
        tile = steps - 1 - s
        heads = [slice(hh * LANES, (hh + 1) * LANES) for hh in range(RET_HEADS)]
        k_decay = [jnp.exp(lgb_ref[:, sl] * row) for sl in heads]
        chunk_decay = [jnp.exp(lgb_ref[:, sl] * C) for sl in heads]
        for c in reversed(range(chunks)):
            rows = slice(c * C, (c + 1) * C)
            for hh, sl in enumerate(heads):
                t_all[tile * chunks + c, hh] = state[hh].astype(BF16)
                kd = (k_ref[rows, sl].astype(F32) * k_decay[hh]).T.astype(BF16)
                kv = jnp.dot(kd, v_ref[rows, sl], preferred_element_type=F32)
                state[hh] = state[hh] * chunk_decay[hh] + kv

    @pl.when(phase == 1)
    def _():
        diff = row - col
        heads = [slice(hh * LANES, (hh + 1) * LANES) for hh in range(RET_HEADS)]
        decay = [jnp.where(diff >= 0, jnp.exp(lgf_ref[:, sl] * jnp.maximum(diff, 0.0)),
                           jnp.exp(lgb_ref[:, sl] * jnp.maximum(-diff, 0.0))) for sl in heads]
        q_decay_f = [jnp.exp(lgf_ref[:, sl] * (row + 1.0)) for sl in heads]
        q_decay_b = [jnp.exp(lgb_ref[:, sl] * (C - row)) for sl in heads]
        k_decay = [jnp.exp(lgf_ref[:, sl] * (C - 1.0 - row)) for sl in heads]
        chunk_decay = [jnp.exp(lgf_ref[:, sl] * C) for sl in heads]
        for c in range(chunks):
            rows = slice(c * C, (c + 1) * C)
            for hh, sl in enumerate(heads):
                q = q_ref[rows, sl]
                k = k_ref[rows, sl]
                v = v_ref[rows, sl]
                a = lax.dot_general(q, k, (((1,), (1,)), ((), ())), preferred_element_type=F32)
                y = jnp.dot((a * decay[hh]).astype(BF16), v, preferred_element_type=F32)
                y = y + q_decay_f[hh] * jnp.dot(q, state[hh].astype(BF16), preferred_element_type=F32)
                y = y + q_decay_b[hh] * jnp.dot(q, t_all[s * chunks + c, hh],
                                                preferred_element_type=F32)
                kd = (k.astype(F32) * k_decay[hh]).T.astype(BF16)
                kv = jnp.dot(kd, v, preferred_element_type=F32)
                state[hh] = state[hh] * chunk_decay[hh] + kv
                g = g_ref[rows, sl].astype(F32)
                gate = g * (1.0 / (1.0 + jnp.exp(-g)))
                o_ref[rows, sl] = (gate * _rms(y, nw_ref[:, sl])).astype(BF16)


def _retention(q, k, v, g, lgf, lgb, nw, batch, seq_len):
    n_tok = q.shape[0]
    steps = seq_len // RET_TILE
    n_chunks = seq_len // RET_CHUNK
    fwd_only = lambda b, p, s: (b * steps + s * p, 0)
    both = lambda b, p, s: (b * steps + jnp.where(p == 0, steps - 1 - s, s), 0)
    blk = lambda f: pl.BlockSpec((RET_TILE, RET_WIDTH), f)
    return pl.pallas_call(
        functools.partial(_ret_kernel, steps=steps),
        out_shape=jax.ShapeDtypeStruct((n_tok, RET_WIDTH), BF16),
        grid=(batch, 2, steps),
        in_specs=[blk(fwd_only), blk(both), blk(both), blk(fwd_only),
                  _const_spec(lgf.shape), _const_spec(lgb.shape), _const_spec(nw.shape)],
        out_specs=blk(fwd_only),
        scratch_shapes=[pltpu.VMEM((n_chunks, RET_HEADS, RET_HEAD_DIM, RET_HEAD_DIM), BF16),
                        pltpu.VMEM((RET_HEADS, RET_HEAD_DIM, RET_HEAD_DIM), F32)],
        compiler_params=pltpu.CompilerParams(
            dimension_semantics=("arbitrary", "arbitrary", "arbitrary"),
            vmem_limit_bytes=VMEM_LIMIT),
        name="retention",
    )(q, k, v, g, lgf, lgb, nw)


def _out_ffn_kernel(x_ref, xp_ref, xn_ref, na_ref, nap_ref, nan_ref, rt_ref, rtp_ref, rtn_ref,
                    wo_ref, n2_ref, wu_ref, cw_ref, cb_ref, wd_ref, o_ref, n_scr, f_scr, act_scr,
                    *, tiles_per_seq, ffn_dim):
    i = pl.program_id(0)
    rows = TOK_TILE + 2 * HALO
    slabs = n_scr.shape[0]
    x = jnp.concatenate([xp_ref[...], x_ref[...], xn_ref[...]], axis=0)
    na = jnp.concatenate([nap_ref[...], na_ref[...], nan_ref[...]], axis=0)
    rt = jnp.concatenate([rtp_ref[...], rt_ref[...], rtn_ref[...]], axis=0)
    x2 = (x + jnp.dot(na, wo_ref[:NA_WIDTH, :], preferred_element_type=F32)
          + jnp.dot(rt, wo_ref[NA_WIDTH:, :], preferred_element_type=F32))
    pos = i % tiles_per_seq
    ridx = lax.broadcasted_iota(jnp.int32, (rows, 1), 0)
    dead = ((ridx < HALO) & (pos == 0)) | ((ridx >= HALO + TOK_TILE) & (pos == tiles_per_seq - 1))
    nf = jnp.where(dead, 0.0, _rms(x2, n2_ref[...]))

    stride = rows // SUBLANES
    for j in range(slabs):
        n_scr[j] = nf[:, j * LANES:(j + 1) * LANES]
    n = jnp.concatenate(
        [jnp.concatenate([n_scr.at[j][pl.ds(v, SUBLANES, stride=stride), :] for j in range(slabs)], axis=1)
         for v in range(stride)], axis=0).astype(BF16)

    n_chunks = ffn_dim // FFN_CHUNK

    def up_proj(c):
        return [jnp.dot(n, wu_ref[:, base + c * FFN_CHUNK:base + (c + 1) * FFN_CHUNK],
                        preferred_element_type=F32) for base in (0, ffn_dim)]

    def conv(h, c, base):
        cols = slice(base + c * FFN_CHUNK, base + (c + 1) * FFN_CHUNK)
        prev = jnp.concatenate([pltpu.roll(h[rows - SUBLANES:], 1, 0), h[:rows - SUBLANES]], axis=0)
        nxt = jnp.concatenate([h[SUBLANES:], pltpu.roll(h[:SUBLANES], SUBLANES - 1, 0)], axis=0)
        return prev * cw_ref[0:1, cols] + h * cw_ref[1:2, cols] + nxt * cw_ref[2:3, cols] + cb_ref[:, cols]

    for c in range(n_chunks):
        h_gate, h_up = up_proj(c)
        gate = conv(h_gate, c, 0)
        up = conv(h_up, c, ffn_dim)
        act_scr[:, c * FFN_CHUNK:(c + 1) * FFN_CHUNK] = (
            gate * (1.0 / (1.0 + jnp.exp(-gate))) * up).astype(BF16)
    ffn = jnp.dot(act_scr[...], wd_ref[...], preferred_element_type=F32)

    for v in range(stride):
        for j in range(slabs):
            f_scr.at[j][pl.ds(v, SUBLANES, stride=stride), :] = (
                ffn[v * SUBLANES:(v + 1) * SUBLANES, j * LANES:(j + 1) * LANES])
    for j in range(slabs):
        cols = slice(j * LANES, (j + 1) * LANES)
        o_ref[:, cols] = f_scr[j, HALO:HALO + TOK_TILE, :] + x2[HALO:HALO + TOK_TILE, cols]


def _out_ffn(x2d, na, rt, seq_len, wo, n2, wu, cw, cb, wd):
    n_tok, d_model = x2d.shape
    ffn_dim = wd.shape[0]
    tiles_per_seq = seq_len // TOK_TILE
    per_tile = TOK_TILE // HALO
    last_halo = n_tok // HALO - 1
    cur = lambda i: (i, 0)
    prv = lambda i: (jnp.maximum(i * per_tile - 1, 0), 0)
    nxt = lambda i: (jnp.minimum((i + 1) * per_tile, last_halo), 0)
    main = lambda w: pl.BlockSpec((TOK_TILE, w), cur)
    hp = lambda w: pl.BlockSpec((HALO, w), prv)
    hn = lambda w: pl.BlockSpec((HALO, w), nxt)
    return pl.pallas_call(
        functools.partial(_out_ffn_kernel, tiles_per_seq=tiles_per_seq, ffn_dim=ffn_dim),
        out_shape=jax.ShapeDtypeStruct((n_tok, d_model), F32),
        grid=(n_tok // TOK_TILE,),
        in_specs=[main(d_model), hp(d_model), hn(d_model),
                  main(NA_WIDTH), hp(NA_WIDTH), hn(NA_WIDTH),
                  main(RET_WIDTH), hp(RET_WIDTH), hn(RET_WIDTH),
                  _const_spec(wo.shape), _const_spec(n2.shape), _const_spec(wu.shape),
                  _const_spec(cw.shape), _const_spec(cb.shape), _const_spec(wd.shape)],
        out_specs=main(d_model),
        scratch_shapes=[pltpu.VMEM((d_model // LANES, TOK_TILE + 2 * HALO, LANES), F32),
                        pltpu.VMEM((d_model // LANES, TOK_TILE + 2 * HALO, LANES), F32),
                        pltpu.VMEM((TOK_TILE + 2 * HALO, ffn_dim), BF16)],
        compiler_params=pltpu.CompilerParams(dimension_semantics=("arbitrary",),
                                             vmem_limit_bytes=VMEM_LIMIT),
        name="out_ffn",
    )(x2d, x2d, x2d, na, na, na, rt, rt, rt, wo, n2, wu, cw, cb, wd)


def _rotary_tables():
    half = RET_HEAD_DIM // 2
    inv_freq = ROPE_BASE ** (-jnp.arange(half, dtype=F32) / half)
    freq = jnp.concatenate([inv_freq, inv_freq])[None, :]
    ang = jnp.arange(IN_TILE, dtype=F32)[:, None] * freq
    return freq, jnp.cos(ang), jnp.sin(ang)


def _layer(x, p):
    batch, seq_len, d_model = x.shape
    x2d = x.reshape(batch * seq_len, d_model)
    naq, nak, nav, rq, rk, rv, rg = _in_proj(
        x2d, seq_len, p["n1"], p["w_in"], p["bd"], p["qn"], p["kn"], p["rope"])
    na = _na(naq, nak, nav, p["bias"], batch, seq_len)
    rt = _retention(rq, rk, rv, rg, p["lgf"], p["lgb"], p["nw"], batch, seq_len)
    y = _out_ffn(x2d, na, rt, seq_len, p["wo"], p["n2"], p["wu"], p["cw"], p["cb"], p["wd"])
    return y.reshape(batch, seq_len, d_model)


def _prepare(norm1_w, w_in, na_q_norm_w, na_k_norm_w, na_rpb, ret_theta_fwd, ret_theta_bwd,
             ret_norm_w, w_out, norm2_w, w_ffn_up, ffn_conv_w, ffn_conv_b, w_ffn_down):
    head = np.arange(NA_WIDTH) // NA_HEAD_DIM
    bd = jnp.asarray((head[:, None] == head[None, :]).astype(np.float32) / NA_HEAD_DIM, BF16)
    lanes = lambda t: jnp.repeat(t.astype(F32), RET_HEAD_DIM)[None, :]
    return dict(
        n1=norm1_w[None, :], w_in=w_in.astype(BF16), bd=bd, rope=_rotary_tables(),
        qn=jnp.tile(na_q_norm_w, NA_HEADS)[None, :], kn=jnp.tile(na_k_norm_w, NA_HEADS)[None, :],
        bias=_na_bias_tables(na_rpb).reshape(3, NA_HEADS // 2, 2 * NA_GROUP_TOK, NA_KEY_TOK),
        lgf=lanes(jax.nn.log_sigmoid(ret_theta_fwd.astype(F32))),
        lgb=lanes(jax.nn.log_sigmoid(ret_theta_bwd.astype(F32))),
        nw=ret_norm_w[None, :], wo=w_out.astype(BF16), n2=norm2_w[None, :],
        wu=w_ffn_up.astype(BF16), cw=ffn_conv_w, cb=ffn_conv_b[None, :], wd=w_ffn_down.astype(BF16))


def kernel(x_prompt, x_sample, norm1_w, w_in, na_q_norm_w, na_k_norm_w, na_rpb, ret_theta_fwd, ret_theta_bwd, ret_norm_w, w_out, norm2_w, w_ffn_up, ffn_conv_w, ffn_conv_b, w_ffn_down):
    depth = norm1_w.shape[0]
    y_prompt, y_sample = x_prompt, x_sample
    for l in range(depth):
        p = _prepare(norm1_w[l], w_in[l], na_q_norm_w[l], na_k_norm_w[l], na_rpb[l],
                     ret_theta_fwd[l], ret_theta_bwd[l], ret_norm_w[l], w_out[l], norm2_w[l],
                     w_ffn_up[l], ffn_conv_w[l], ffn_conv_b[l], w_ffn_down[l])
        y_prompt = _layer(y_prompt, p)
        y_sample = _layer(y_sample, p)
    return (y_prompt, y_sample)
```

```python
import functools

import jax
import jax.numpy as jnp
import numpy as np
from jax import lax
from jax.experimental import pallas as pl
from jax.experimental.pallas import tpu as pltpu

F32 = jnp.float32
BF16 = jnp.bfloat16

GRID_W = 64
NA_HEADS = 8
NA_HEAD_DIM = 64
NA_WIDTH = NA_HEADS * NA_HEAD_DIM
NA_WIN_R = 8
NA_WIN_C = 16
RET_HEADS = 4
RET_HEAD_DIM = 128
RET_WIDTH = RET_HEADS * RET_HEAD_DIM
RET_CHUNK = 128
ROPE_BASE = 10000.0
CONV_W = 3
NORM_EPS = 1e-6

LANES = 128
SUBLANES = 8
BF16_ROWS = 16
IN_TILE = 1024
NA_TILE = 1024
RET_TILE = 2048
TOK_TILE = 512
NA_ROWS_PER_STEP = NA_TILE // GRID_W
NA_GROUP_ROWS = 4
NA_KEY_ROWS = 12
NA_GROUP_TOK = NA_GROUP_ROWS * GRID_W
NA_KEY_TOK = NA_KEY_ROWS * GRID_W
MASK_VALUE = -1e30
FFN_CHUNK = 256
HALO = SUBLANES
VMEM_LIMIT = 56 * 1024 * 1024


def _const_spec(shape):
    return pl.BlockSpec(shape, lambda *_: (0,) * len(shape), pipeline_mode=pl.Buffered(1))


def _rms(x, w):
    ms = jnp.mean(x * x, axis=-1, keepdims=True)
    return x * lax.rsqrt(ms + NORM_EPS) * w


def _in_proj_kernel(x_ref, n1_ref, w_ref, bd_ref, qn_ref, kn_ref, freq_ref, cosr_ref, sinr_ref,
                    naq_ref, nak_ref, nav_ref, rq_ref, rk_ref, rv_ref, rg_ref, *, tiles_per_seq):
    h = _rms(x_ref[...], n1_ref[...]).astype(BF16)

    def proj(idx):
        lo = idx * NA_WIDTH
        return jnp.dot(h, w_ref[:, lo:lo + NA_WIDTH], preferred_element_type=F32)

    def head_norm(t, w_row, scale):
        ms = jnp.dot((t * t).astype(BF16), bd_ref[...], preferred_element_type=F32)
        return t * lax.rsqrt(ms + NORM_EPS) * w_row * scale

    naq_ref[...] = head_norm(proj(0), qn_ref[...], NA_HEAD_DIM ** -0.5).astype(BF16)
    nak_ref[...] = head_norm(proj(1), kn_ref[...], 1.0).astype(BF16)
    nav_ref[...] = proj(2).astype(BF16)

    first = ((pl.program_id(0) % tiles_per_seq) * IN_TILE).astype(F32)
    a0 = jnp.broadcast_to(first * freq_ref[...], (SUBLANES, LANES))
    c0 = jnp.cos(a0)[0:1, :]
    s0 = jnp.sin(a0)[0:1, :]
    cos_r = cosr_ref[...]
    sin_r = sinr_ref[...]
    lower = lax.broadcasted_iota(jnp.int32, (IN_TILE, LANES), 1) < RET_HEAD_DIM // 2
    cs = c0 * cos_r - s0 * sin_r
    sin = s0 * cos_r + c0 * sin_r
    sn = jnp.where(lower, -sin, sin)

    def rotary(t, scale):
        outs = []
        for hh in range(RET_HEADS):
            th = t[:, hh * LANES:(hh + 1) * LANES]
            r = th * cs + pltpu.roll(th, RET_HEAD_DIM // 2, 1) * sn
            outs.append(r * scale if scale != 1.0 else r)
        return jnp.concatenate(outs, axis=1)

    rq_ref[...] = rotary(proj(3), 1.0).astype(BF16)
    rk_ref[...] = rotary(proj(4), RET_HEAD_DIM ** -0.5).astype(BF16)
    rv_ref[...] = proj(5).astype(BF16)
    rg_ref[...] = proj(6).astype(BF16)


def _in_proj(x2d, seq_len, n1, w_in, bd, qn, kn, rope):
    freq, cos_r, sin_r = rope
    n_tok, d_model = x2d.shape
    out = jax.ShapeDtypeStruct((n_tok, NA_WIDTH), BF16)
    tok_spec = lambda width: pl.BlockSpec((IN_TILE, width), lambda i: (i, 0))
    return pl.pallas_call(
        functools.partial(_in_proj_kernel, tiles_per_seq=seq_len // IN_TILE),
        out_shape=(out,) * 7,
        grid=(n_tok // IN_TILE,),
        in_specs=[tok_spec(d_model), _const_spec(n1.shape), _const_spec(w_in.shape),
                  _const_spec(bd.shape), _const_spec(qn.shape), _const_spec(kn.shape),
                  _const_spec(freq.shape), _const_spec(cos_r.shape), _const_spec(sin_r.shape)],
        out_specs=(tok_spec(NA_WIDTH),) * 7,
        compiler_params=pltpu.CompilerParams(dimension_semantics=("arbitrary",),
                                             vmem_limit_bytes=VMEM_LIMIT),
        name="in_proj",
    )(x2d, n1, w_in, bd, qn, kn, freq, cos_r, sin_r)


def _na_kernel(q_ref, kp_ref, kc_ref, kn_ref, vp_ref, vc_ref, vn_ref, bias_ref, o_ref,
               kbuf, vbuf, *, grid_rows):
    r = pl.program_id(1)
    kbuf[0:NA_GROUP_TOK, :] = kp_ref[...]
    kbuf[NA_GROUP_TOK:NA_GROUP_TOK + NA_TILE, :] = kc_ref[...]
    kbuf[NA_GROUP_TOK + NA_TILE:, :] = kn_ref[...]
    vbuf[0:NA_GROUP_TOK, :] = vp_ref[...]
    vbuf[NA_GROUP_TOK:NA_GROUP_TOK + NA_TILE, :] = vc_ref[...]
    vbuf[NA_GROUP_TOK + NA_TILE:, :] = vn_ref[...]

    lane = lax.broadcasted_iota(jnp.int32, (NA_GROUP_TOK, LANES), 1)
    low_half = lane < NA_HEAD_DIM

    for g in range(NA_ROWS_PER_STEP // NA_GROUP_ROWS):
        row0 = r * NA_ROWS_PER_STEP + g * NA_GROUP_ROWS
        is_first = row0 == 0
        is_last = row0 == grid_rows - NA_GROUP_ROWS
        cls = jnp.where(is_first, 0, jnp.where(is_last, 2, 1))
        win_row = jnp.where(is_first, 0, jnp.where(is_last, grid_rows - NA_KEY_ROWS, row0 - NA_GROUP_ROWS))
        start = pl.multiple_of(
            (win_row - (r * NA_ROWS_PER_STEP - NA_GROUP_ROWS)) * GRID_W, NA_GROUP_TOK)
        kwin = kbuf[pl.ds(start, NA_KEY_TOK), :]
        vwin = vbuf[pl.ds(start, NA_KEY_TOK), :]
        qg = q_ref[g * NA_GROUP_TOK:(g + 1) * NA_GROUP_TOK, :]
        for p in range(NA_HEADS // 2):
            sl = slice(p * LANES, (p + 1) * LANES)
            q2 = qg[:, sl]
            k2 = kwin[:, sl]
            v2 = vwin[:, sl]
            zero = jnp.zeros_like(q2)
            qs = jnp.concatenate([jnp.where(low_half, q2, zero), jnp.where(low_half, zero, q2)], axis=0)
            s = lax.dot_general(qs, k2, (((1,), (1,)), ((), ())), preferred_element_type=F32)
            s = s + bias_ref[cls, p]
            m = jnp.max(s, axis=-1, keepdims=True)
            e = jnp.exp(s - m)
            l = jnp.sum(e, axis=-1, keepdims=True)
            o2 = jnp.dot(e.astype(BF16), v2, preferred_element_type=F32) / l
            o_ref[g * NA_GROUP_TOK:(g + 1) * NA_GROUP_TOK, sl] = (
                jnp.where(low_half, o2[:NA_GROUP_TOK], o2[NA_GROUP_TOK:]).astype(BF16))


def _na_bias_kernel(w_ref, o_ref):
    c = lax.broadcasted_iota(jnp.int32, (GRID_W, LANES), 0)
    lane = lax.broadcasted_iota(jnp.int32, (GRID_W, LANES), 1)
    kc = lane & (GRID_W - 1)
    col_start = jnp.clip(c - NA_WIN_C // 2, 0, GRID_W - NA_WIN_C)
    col_ok = (kc >= col_start) & (kc < col_start + NA_WIN_C)
    low = lane < GRID_W
    masked = jnp.full((GRID_W, LANES), MASK_VALUE, F32)

    def toeplitz(dr, lane_off):
        row = jnp.broadcast_to(w_ref[0, dr:dr + 1, :], (GRID_W, LANES))
        return pltpu.roll(row, (LANES - (NA_WIN_C - 1) + lane_off) % LANES, 1, stride=1, stride_axis=0)

    rel_offset = (0, -NA_GROUP_ROWS, -2 * NA_GROUP_ROWS)
    for cls in range(3):
        for a in range(NA_GROUP_ROWS):
            for bp in range(NA_KEY_ROWS // 2):
                halves = []
                for b in (2 * bp, 2 * bp + 1):
                    row_ok = (b < NA_WIN_R, 0 <= b - a < NA_WIN_R, b >= NA_KEY_ROWS - NA_WIN_R)[cls]
                    dr = b - a + rel_offset[cls] + NA_WIN_R - 1
                    halves.append(toeplitz(dr, (b % 2) * GRID_W) if row_ok else None)
                left, right = halves
                if left is None and right is None:
                    tile = masked
                else:
                    ok = col_ok
                    if left is None:
                        left, ok = masked, ok & jnp.logical_not(low)
                    if right is None:
                        right, ok = masked, ok & low
                    tile = jnp.where(ok, jnp.where(low, left, right), MASK_VALUE)
                o_ref[cls, 0, a * GRID_W:(a + 1) * GRID_W, bp * LANES:(bp + 1) * LANES] = tile


def _na_bias_tables(rpb):
    heads, n_dr, n_dc = rpb.shape
    w = jnp.pad(rpb.astype(F32), ((0, 0), (0, 0), (0, LANES - n_dc)))
    return pl.pallas_call(
        _na_bias_kernel,
        out_shape=jax.ShapeDtypeStruct((3, heads, NA_GROUP_TOK, NA_KEY_TOK), F32),
        grid=(heads,),
        in_specs=[pl.BlockSpec((1, n_dr, LANES), lambda h: (h, 0, 0))],
        out_specs=pl.BlockSpec((3, 1, NA_GROUP_TOK, NA_KEY_TOK), lambda h: (0, h, 0, 0)),
        compiler_params=pltpu.CompilerParams(dimension_semantics=("arbitrary",),
                                             vmem_limit_bytes=VMEM_LIMIT),
        name="na_bias",
    )(w)


def _na(q, k, v, bias, batch, seq_len):
    n_tok = q.shape[0]
    grid_rows = seq_len // GRID_W
    nb = seq_len // NA_TILE
    per = NA_TILE // NA_GROUP_TOK
    nh = seq_len // NA_GROUP_TOK
    cur = lambda b, r: (b * nb + r, 0)
    prv = lambda b, r: (b * nh + jnp.maximum(r * per - 1, 0), 0)
    nxt = lambda b, r: (b * nh + jnp.minimum((r + 1) * per, nh - 1), 0)
    blk = lambda f: pl.BlockSpec((NA_TILE, NA_WIDTH), f)
    halo = lambda f: pl.BlockSpec((NA_GROUP_TOK, NA_WIDTH), f)
    return pl.pallas_call(
        functools.partial(_na_kernel, grid_rows=grid_rows),
        out_shape=jax.ShapeDtypeStruct((n_tok, NA_WIDTH), BF16),
        grid=(batch, nb),
        in_specs=[blk(cur), halo(prv), blk(cur), halo(nxt), halo(prv), blk(cur), halo(nxt),
                  _const_spec(bias.shape)],
        out_specs=blk(cur),
        scratch_shapes=[pltpu.VMEM((NA_TILE + 2 * NA_GROUP_TOK, NA_WIDTH), BF16),
                        pltpu.VMEM((NA_TILE + 2 * NA_GROUP_TOK, NA_WIDTH), BF16)],
        compiler_params=pltpu.CompilerParams(dimension_semantics=("arbitrary", "arbitrary"),
                                             vmem_limit_bytes=VMEM_LIMIT),
        name="na",
    )(q, k, k, k, v, v, v, bias)


def _ret_kernel(q_ref, k_ref, v_ref, g_ref, lgf_ref, lgb_ref, nw_ref, o_ref, t_all, state,
                *, steps):
    phase = pl.program_id(1)
    s = pl.program_id(2)
    chunks = RET_TILE // RET_CHUNK
    C = RET_CHUNK

    @pl.when(s == 0)
    def _():
        state[...] = jnp.zeros_like(state)

    row = lax.broadcasted_iota(jnp.int32, (C, LANES), 0).astype(F32)
    col = lax.broadcasted_iota(jnp.int32, (C, LANES), 1).astype(F32)

    @pl.when(phase == 0)
    def _():
        tile = steps - 1 - s
        heads = [slice(hh * LANES, (hh + 1) * LANES) for hh in range(RET_HEADS)]
        k_decay = [jnp.exp(lgb_ref[:, sl] * row) for sl in heads]
        chunk_decay = [jnp.exp(lgb_ref[:, sl] * C) for sl in heads]
        for c in reversed(range(chunks)):
            rows = slice(c * C, (c + 1) * C)
            for hh, sl in enumerate(heads):
                t_all[tile * chunks + c, hh] = state[hh].astype(BF16)
                kd = (k_ref[rows, sl].astype(F32) * k_decay[hh]).T.astype(BF16)
                kv = jnp.dot(kd, v_ref[rows, sl], preferred_element_type=F32)
                state[hh] = state[hh] * chunk_decay[hh] + kv

    @pl.when(phase == 1)
    def _():
        diff = row - col
        heads = [slice(hh * LANES, (hh + 1) * LANES) for hh in range(RET_HEADS)]
        decay = [jnp.where(diff >= 0, jnp.exp(lgf_ref[:, sl] * jnp.maximum(diff, 0.0)),
                           jnp.exp(lgb_ref[:, sl] * jnp.maximum(-diff, 0.0))) for sl in heads]
        q_decay_f = [jnp.exp(lgf_ref[:, sl] * (row + 1.0)) for sl in heads]
        q_decay_b = [jnp.exp(lgb_ref[:, sl] * (C - row)) for sl in heads]
        k_decay = [jnp.exp(lgf_ref[:, sl] * (C - 1.0 - row)) for sl in heads]
        chunk_decay = [jnp.exp(lgf_ref[:, sl] * C) for sl in heads]
        for c in range(chunks):
            rows = slice(c * C, (c + 1) * C)
            for hh, sl in enumerate(heads):
                q = q_ref[rows, sl]
                k = k_ref[rows, sl]
                v = v_ref[rows, sl]
                a = lax.dot_general(q, k, (((1,), (1,)), ((), ())), preferred_element_type=F32)
                y = jnp.dot((a * decay[hh]).astype(BF16), v, preferred_element_type=F32)
                y = y + q_decay_f[hh] * jnp.dot(q, state[hh].astype(BF16), preferred_element_type=F32)
                y = y + q_decay_b[hh] * jnp.dot(q, t_all[s * chunks + c, hh],
                                                preferred_element_type=F32)
                kd = (k.astype(F32) * k_decay[hh]).T.astype(BF16)
                kv = jnp.dot(kd, v, preferred_element_type=F32)
                state[hh] = state[hh] * chunk_decay[hh] + kv
                g = g_ref[rows, sl].astype(F32)
                gate = g * (1.0 / (1.0 + jnp.exp(-g)))
                o_ref[rows, sl] = (gate * _rms(y, nw_ref[:, sl])).astype(BF16)


def _retention(q, k, v, g, lgf, lgb, nw, batch, seq_len):
    n_tok = q.shape[0]
    steps = seq_len // RET_TILE
    n_chunks = seq_len // RET_CHUNK
    fwd_only = lambda b, p, s: (b * steps + s * p, 0)
    both = lambda b, p, s: (b * steps + jnp.where(p == 0, steps - 1 - s, s), 0)
    blk = lambda f: pl.BlockSpec((RET_TILE, RET_WIDTH), f)
    return pl.pallas_call(
        functools.partial(_ret_kernel, steps=steps),
        out_shape=jax.ShapeDtypeStruct((n_tok, RET_WIDTH), BF16),
        grid=(batch, 2, steps),
        in_specs=[blk(fwd_only), blk(both), blk(both), blk(fwd_only),
                  _const_spec(lgf.shape), _const_spec(lgb.shape), _const_spec(nw.shape)],
        out_specs=blk(fwd_only),
        scratch_shapes=[pltpu.VMEM((n_chunks, RET_HEADS, RET_HEAD_DIM, RET_HEAD_DIM), BF16),
                        pltpu.VMEM((RET_HEADS, RET_HEAD_DIM, RET_HEAD_DIM), F32)],
        compiler_params=pltpu.CompilerParams(
            dimension_semantics=("arbitrary", "arbitrary", "arbitrary"),
            vmem_limit_bytes=VMEM_LIMIT),
        name="retention",
    )(q, k, v, g, lgf, lgb, nw)


def _out_ffn_kernel(x_ref, xp_ref, xn_ref, na_ref, nap_ref, nan_ref, rt_ref, rtp_ref, rtn_ref,
                    wo_ref, n2_ref, wu_ref, cw_ref, cb_ref, wd_ref, o_ref, n_scr, f_scr, act_scr,
                    *, tiles_per_seq, ffn_dim):
    i = pl.program_id(0)
    rows = TOK_TILE + 2 * HALO
    slabs = n_scr.shape[0]
    x = jnp.concatenate([xp_ref[...], x_ref[...], xn_ref[...]], axis=0)

    def with_halo(prev_ref, cur_ref, next_ref):
        return jnp.concatenate([prev_ref[...].astype(F32)[BF16_ROWS - HALO:], cur_ref[...].astype(F32),
                                next_ref[...].astype(F32)[:HALO]], axis=0).astype(BF16)

    mix = jnp.concatenate([with_halo(nap_ref, na_ref, nan_ref), with_halo(rtp_ref, rt_ref, rtn_ref)],
                          axis=1)
    x2 = x + jnp.dot(mix, wo_ref[...], preferred_element_type=F32)
    pos = i % tiles_per_seq
    ridx = lax.broadcasted_iota(jnp.int32, (rows, 1), 0)
    dead = ((ridx < HALO) & (pos == 0)) | ((ridx >= HALO + TOK_TILE) & (pos == tiles_per_seq - 1))
    nf = jnp.where(dead, 0.0, _rms(x2, n2_ref[...]))

    stride = rows // SUBLANES
    for j in range(slabs):
        n_scr[j] = nf[:, j * LANES:(j + 1) * LANES]
    n = jnp.concatenate(
        [jnp.concatenate([n_scr.at[j][pl.ds(v, SUBLANES, stride=stride), :] for j in range(slabs)], axis=1)
         for v in range(stride)], axis=0).astype(BF16)

    n_chunks = ffn_dim // FFN_CHUNK

    def up_proj(c):
        return [jnp.dot(n, wu_ref[:, base + c * FFN_CHUNK:base + (c + 1) * FFN_CHUNK],
                        preferred_element_type=F32) for base in (0, ffn_dim)]

    def conv(h, c, base):
        cols = slice(base + c * FFN_CHUNK, base + (c + 1) * FFN_CHUNK)
        prev = jnp.concatenate([pltpu.roll(h[rows - SUBLANES:], 1, 0), h[:rows - SUBLANES]], axis=0)
        nxt = jnp.concatenate([h[SUBLANES:], pltpu.roll(h[:SUBLANES], SUBLANES - 1, 0)], axis=0)
        return prev * cw_ref[0:1, cols] + h * cw_ref[1:2, cols] + nxt * cw_ref[2:3, cols] + cb_ref[:, cols]

    for c in range(n_chunks):
        h_gate, h_up = up_proj(c)
        gate = conv(h_gate, c, 0)
        up = conv(h_up, c, ffn_dim)
        act_scr[:, c * FFN_CHUNK:(c + 1) * FFN_CHUNK] = (
            gate * (1.0 / (1.0 + jnp.exp(-gate))) * up).astype(BF16)
    ffn = jnp.dot(act_scr[...], wd_ref[...], preferred_element_type=F32)

    for v in range(stride):
        for j in range(slabs):
            f_scr.at[j][pl.ds(v, SUBLANES, stride=stride), :] = (
                ffn[v * SUBLANES:(v + 1) * SUBLANES, j * LANES:(j + 1) * LANES])
    for j in range(slabs):
        cols = slice(j * LANES, (j + 1) * LANES)
        o_ref[:, cols] = f_scr[j, HALO:HALO + TOK_TILE, :] + x2[HALO:HALO + TOK_TILE, cols]


def _out_ffn(x2d, na, rt, seq_len, wo, n2, wu, cw, cb, wd):
    n_tok, d_model = x2d.shape
    ffn_dim = wd.shape[0]
    tiles_per_seq = seq_len // TOK_TILE
    cur = lambda i: (i, 0)
    main = lambda w: pl.BlockSpec((TOK_TILE, w), cur)

    def halo_specs(width, block_rows):
        per_tile = TOK_TILE // block_rows
        last = n_tok // block_rows - 1
        return (pl.BlockSpec((block_rows, width), lambda i: (jnp.maximum(i * per_tile - 1, 0), 0)),
                pl.BlockSpec((block_rows, width), lambda i: (jnp.minimum((i + 1) * per_tile, last), 0)))

    return pl.pallas_call(
        functools.partial(_out_ffn_kernel, tiles_per_seq=tiles_per_seq, ffn_dim=ffn_dim),
        out_shape=jax.ShapeDtypeStruct((n_tok, d_model), F32),
        grid=(n_tok // TOK_TILE,),
        in_specs=[main(d_model), *halo_specs(d_model, HALO),
                  main(NA_WIDTH), *halo_specs(NA_WIDTH, BF16_ROWS),
                  main(RET_WIDTH), *halo_specs(RET_WIDTH, BF16_ROWS),
                  _const_spec(wo.shape), _const_spec(n2.shape), _const_spec(wu.shape),
                  _const_spec(cw.shape), _const_spec(cb.shape), _const_spec(wd.shape)],
        out_specs=main(d_model),
        scratch_shapes=[pltpu.VMEM((d_model // LANES, TOK_TILE + 2 * HALO, LANES), F32),
                        pltpu.VMEM((d_model // LANES, TOK_TILE + 2 * HALO, LANES), F32),
                        pltpu.VMEM((TOK_TILE + 2 * HALO, ffn_dim), BF16)],
        compiler_params=pltpu.CompilerParams(dimension_semantics=("arbitrary",),
                                             vmem_limit_bytes=VMEM_LIMIT),
        name="out_ffn",
    )(x2d, x2d, x2d, na, na, na, rt, rt, rt, wo, n2, wu, cw, cb, wd)


def _rotary_tables():
    half = RET_HEAD_DIM // 2
    inv_freq = ROPE_BASE ** (-jnp.arange(half, dtype=F32) / half)
    freq = jnp.concatenate([inv_freq, inv_freq])[None, :]
    ang = jnp.arange(IN_TILE, dtype=F32)[:, None] * freq
    return freq, jnp.cos(ang), jnp.sin(ang)


def _layer(x, p):
    batch, seq_len, d_model = x.shape
    x2d = x.reshape(batch * seq_len, d_model)
    naq, nak, nav, rq, rk, rv, rg = _in_proj(
        x2d, seq_len, p["n1"], p["w_in"], p["bd"], p["qn"], p["kn"], p["rope"])
    na = _na(naq, nak, nav, p["bias"], batch, seq_len)
    rt = _retention(rq, rk, rv, rg, p["lgf"], p["lgb"], p["nw"], batch, seq_len)
    y = _out_ffn(x2d, na, rt, seq_len, p["wo"], p["n2"], p["wu"], p["cw"], p["cb"], p["wd"])
    return y.reshape(batch, seq_len, d_model)


def _prepare(norm1_w, w_in, na_q_norm_w, na_k_norm_w, na_rpb, ret_theta_fwd, ret_theta_bwd,
             ret_norm_w, w_out, norm2_w, w_ffn_up, ffn_conv_w, ffn_conv_b, w_ffn_down):
    head = np.arange(NA_WIDTH) // NA_HEAD_DIM
    bd = jnp.asarray((head[:, None] == head[None, :]).astype(np.float32) / NA_HEAD_DIM, BF16)
    lanes = lambda t: jnp.repeat(t.astype(F32), RET_HEAD_DIM)[None, :]
    return dict(
        n1=norm1_w[None, :], w_in=w_in.astype(BF16), bd=bd, rope=_rotary_tables(),
        qn=jnp.tile(na_q_norm_w, NA_HEADS)[None, :], kn=jnp.tile(na_k_norm_w, NA_HEADS)[None, :],
        bias=_na_bias_tables(na_rpb).reshape(3, NA_HEADS // 2, 2 * NA_GROUP_TOK, NA_KEY_TOK),
        lgf=lanes(jax.nn.log_sigmoid(ret_theta_fwd.astype(F32))),
        lgb=lanes(jax.nn.log_sigmoid(ret_theta_bwd.astype(F32))),
        nw=ret_norm_w[None, :], wo=w_out.astype(BF16), n2=norm2_w[None, :],
        wu=w_ffn_up.astype(BF16), cw=ffn_conv_w, cb=ffn_conv_b[None, :], wd=w_ffn_down.astype(BF16))


def kernel(x_prompt, x_sample, norm1_w, w_in, na_q_norm_w, na_k_norm_w, na_rpb, ret_theta_fwd, ret_theta_bwd, ret_norm_w, w_out, norm2_w, w_ffn_up, ffn_conv_w, ffn_conv_b, w_ffn_down):
    depth = norm1_w.shape[0]
    y_prompt, y_sample = x_prompt, x_sample
    for l in range(depth):
        p = _prepare(norm1_w[l], w_in[l], na_q_norm_w[l], na_k_norm_w[l], na_rpb[l],
                     ret_theta_fwd[l], ret_theta_bwd[l], ret_norm_w[l], w_out[l], norm2_w[l],
                     w_ffn_up[l], ffn_conv_w[l], ffn_conv_b[l], w_ffn_down[l])
        y_prompt = _layer(y_prompt, p)
        y_sample = _layer(y_sample, p)
    return (y_prompt, y_sample)
```

```python
import functools

import jax
import jax.numpy as jnp
from jax import lax
from jax.experimental import pallas as pl
from jax.experimental.pallas import tpu as pltpu

F32 = jnp.float32
BF16 = jnp.bfloat16

GRID_W = 64
NA_HEADS = 8
NA_HEAD_DIM = 64
NA_WIDTH = NA_HEADS * NA_HEAD_DIM
NA_WIN_R = 8
NA_WIN_C = 16
RET_HEADS = 4
RET_HEAD_DIM = 128
RET_WIDTH = RET_HEADS * RET_HEAD_DIM
RET_CHUNK = 128
ROPE_BASE = 10000.0
CONV_W = 3
NORM_EPS = 1e-6

LANES = 128
SUBLANES = 8
BF16_ROWS = 16
IN_TILE = 1024
NA_TILE = 1024
RET_TILE = 2048
TOK_TILE = 512
NA_ROWS_PER_STEP = NA_TILE // GRID_W
NA_GROUP_ROWS = 4
NA_KEY_ROWS = 12
NA_GROUP_TOK = NA_GROUP_ROWS * GRID_W
NA_KEY_TOK = NA_KEY_ROWS * GRID_W
MASK_VALUE = -1e30
FFN_CHUNK = 256
HALO = SUBLANES
VMEM_LIMIT = 56 * 1024 * 1024


def _const_spec(shape):
    return pl.BlockSpec(shape, lambda *_: (0,) * len(shape), pipeline_mode=pl.Buffered(1))


def _rms(x, w):
    ms = jnp.mean(x * x, axis=-1, keepdims=True)
    return x * lax.rsqrt(ms + NORM_EPS) * w


def _in_proj_kernel(x_ref, n1_ref, w_ref, qn_ref, kn_ref, freq_ref, cosr_ref, sinr_ref,
                    naq_ref, nak_ref, nav_ref, rq_ref, rk_ref, rv_ref, rg_ref, *, tiles_per_seq):
    h = _rms(x_ref[...], n1_ref[...]).astype(BF16)

    def proj(idx):
        lo = idx * NA_WIDTH
        return jnp.dot(h, w_ref[:, lo:lo + NA_WIDTH], preferred_element_type=F32)

    low = lax.broadcasted_iota(jnp.int32, (IN_TILE, LANES), 1) < NA_HEAD_DIM

    def head_norm(t, w_row, scale):
        sq = t * t
        parts = []
        for j in range(NA_WIDTH // LANES):
            s = sq[:, j * LANES:(j + 1) * LANES]
            lo = jnp.sum(jnp.where(low, s, 0.0), axis=-1, keepdims=True)
            hi = jnp.sum(jnp.where(low, 0.0, s), axis=-1, keepdims=True)
            parts.append(jnp.where(low, lo, hi))
        ms = jnp.concatenate(parts, axis=1) * (1.0 / NA_HEAD_DIM)
        return t * lax.rsqrt(ms + NORM_EPS) * w_row * scale

    naq_ref[...] = head_norm(proj(0), qn_ref[...], NA_HEAD_DIM ** -0.5).astype(BF16)
    nak_ref[...] = head_norm(proj(1), kn_ref[...], 1.0).astype(BF16)
    nav_ref[...] = proj(2).astype(BF16)

    first = ((pl.program_id(0) % tiles_per_seq) * IN_TILE).astype(F32)
    a0 = jnp.broadcast_to(first * freq_ref[...], (SUBLANES, LANES))
    c0 = jnp.cos(a0)[0:1, :]
    s0 = jnp.sin(a0)[0:1, :]
    cos_r = cosr_ref[...]
    sin_r = sinr_ref[...]
    lower = lax.broadcasted_iota(jnp.int32, (IN_TILE, LANES), 1) < RET_HEAD_DIM // 2
    cs = c0 * cos_r - s0 * sin_r
    sin = s0 * cos_r + c0 * sin_r
    sn = jnp.where(lower, -sin, sin)

    def rotary(t, scale):
        outs = []
        for hh in range(RET_HEADS):
            th = t[:, hh * LANES:(hh + 1) * LANES]
            r = th * cs + pltpu.roll(th, RET_HEAD_DIM // 2, 1) * sn
            outs.append(r * scale if scale != 1.0 else r)
        return jnp.concatenate(outs, axis=1)

    rq_ref[...] = rotary(proj(3), 1.0).astype(BF16)
    rk_ref[...] = rotary(proj(4), RET_HEAD_DIM ** -0.5).astype(BF16)
    rv_ref[...] = proj(5).astype(BF16)
    rg_ref[...] = proj(6).astype(BF16)


def _in_proj(x2d, seq_len, n1, w_in, qn, kn, rope):
    freq, cos_r, sin_r = rope
    n_tok, d_model = x2d.shape
    out = jax.ShapeDtypeStruct((n_tok, NA_WIDTH), BF16)
    tok_spec = lambda width: pl.BlockSpec((IN_TILE, width), lambda i: (i, 0))
    return pl.pallas_call(
        functools.partial(_in_proj_kernel, tiles_per_seq=seq_len // IN_TILE),
        out_shape=(out,) * 7,
        grid=(n_tok // IN_TILE,),
        in_specs=[tok_spec(d_model), _const_spec(n1.shape), _const_spec(w_in.shape),
                  _const_spec(qn.shape), _const_spec(kn.shape),
                  _const_spec(freq.shape), _const_spec(cos_r.shape), _const_spec(sin_r.shape)],
        out_specs=(tok_spec(NA_WIDTH),) * 7,
        compiler_params=pltpu.CompilerParams(dimension_semantics=("arbitrary",),
                                             vmem_limit_bytes=VMEM_LIMIT),
        name="in_proj",
    )(x2d, n1, w_in, qn, kn, freq, cos_r, sin_r)


def _na_kernel(q_ref, kp_ref, kc_ref, kn_ref, vp_ref, vc_ref, vn_ref, bias_ref, o_ref,
               kbuf, vbuf, *, grid_rows):
    r = pl.program_id(1)
    kbuf[0:NA_GROUP_TOK, :] = kp_ref[...]
    kbuf[NA_GROUP_TOK:NA_GROUP_TOK + NA_TILE, :] = kc_ref[...]
    kbuf[NA_GROUP_TOK + NA_TILE:, :] = kn_ref[...]
    vbuf[0:NA_GROUP_TOK, :] = vp_ref[...]
    vbuf[NA_GROUP_TOK:NA_GROUP_TOK + NA_TILE, :] = vc_ref[...]
    vbuf[NA_GROUP_TOK + NA_TILE:, :] = vn_ref[...]

    lane = lax.broadcasted_iota(jnp.int32, (NA_GROUP_TOK, LANES), 1)
    low_half = lane < NA_HEAD_DIM

    for g in range(NA_ROWS_PER_STEP // NA_GROUP_ROWS):
        row0 = r * NA_ROWS_PER_STEP + g * NA_GROUP_ROWS
        is_first = row0 == 0
        is_last = row0 == grid_rows - NA_GROUP_ROWS
        cls = jnp.where(is_first, 0, jnp.where(is_last, 2, 1))
        win_row = jnp.where(is_first, 0, jnp.where(is_last, grid_rows - NA_KEY_ROWS, row0 - NA_GROUP_ROWS))
        start = pl.multiple_of(
            (win_row - (r * NA_ROWS_PER_STEP - NA_GROUP_ROWS)) * GRID_W, NA_GROUP_TOK)
        kwin = kbuf[pl.ds(start, NA_KEY_TOK), :]
        vwin = vbuf[pl.ds(start, NA_KEY_TOK), :]
        qg = q_ref[g * NA_GROUP_TOK:(g + 1) * NA_GROUP_TOK, :]
        for p in range(NA_HEADS // 2):
            sl = slice(p * LANES, (p + 1) * LANES)
            q2 = qg[:, sl]
            k2 = kwin[:, sl]
            v2 = vwin[:, sl]
            zero = jnp.zeros_like(q2)
            qs = jnp.concatenate([jnp.where(low_half, q2, zero), jnp.where(low_half, zero, q2)], axis=0)
            s = lax.dot_general(qs, k2, (((1,), (1,)), ((), ())), preferred_element_type=F32)
            s = s + bias_ref[cls, p]
            m = jnp.max(s, axis=-1, keepdims=True)
            e = jnp.exp(s - m)
            l = jnp.sum(e, axis=-1, keepdims=True)
            o2 = jnp.dot(e.astype(BF16), v2, preferred_element_type=F32) / l
            o_ref[g * NA_GROUP_TOK:(g + 1) * NA_GROUP_TOK, sl] = (
                jnp.where(low_half, o2[:NA_GROUP_TOK], o2[NA_GROUP_TOK:]).astype(BF16))


def _na_bias_kernel(w_ref, o_ref):
    c = lax.broadcasted_iota(jnp.int32, (GRID_W, LANES), 0)
    lane = lax.broadcasted_iota(jnp.int32, (GRID_W, LANES), 1)
    kc = lane & (GRID_W - 1)
    col_start = jnp.clip(c - NA_WIN_C // 2, 0, GRID_W - NA_WIN_C)
    col_ok = (kc >= col_start) & (kc < col_start + NA_WIN_C)
    low = lane < GRID_W
    masked = jnp.full((GRID_W, LANES), MASK_VALUE, F32)

    def toeplitz(dr, lane_off):
        row = jnp.broadcast_to(w_ref[0, dr:dr + 1, :], (GRID_W, LANES))
        return pltpu.roll(row, (LANES - (NA_WIN_C - 1) + lane_off) % LANES, 1, stride=1, stride_axis=0)

    rel_offset = (0, -NA_GROUP_ROWS, -2 * NA_GROUP_ROWS)
    for cls in range(3):
        for a in range(NA_GROUP_ROWS):
            for bp in range(NA_KEY_ROWS // 2):
                halves = []
                for b in (2 * bp, 2 * bp + 1):
                    row_ok = (b < NA_WIN_R, 0 <= b - a < NA_WIN_R, b >= NA_KEY_ROWS - NA_WIN_R)[cls]
                    dr = b - a + rel_offset[cls] + NA_WIN_R - 1
                    halves.append(toeplitz(dr, (b % 2) * GRID_W) if row_ok else None)
                left, right = halves
                if left is None and right is None:
                    tile = masked
                else:
                    ok = col_ok
                    if left is None:
                        left, ok = masked, ok & jnp.logical_not(low)
                    if right is None:
                        right, ok = masked, ok & low
                    tile = jnp.where(ok, jnp.where(low, left, right), MASK_VALUE)
                o_ref[cls, 0, a * GRID_W:(a + 1) * GRID_W, bp * LANES:(bp + 1) * LANES] = tile


def _na_bias_tables(rpb):
    heads, n_dr, n_dc = rpb.shape
    w = jnp.pad(rpb.astype(F32), ((0, 0), (0, 0), (0, LANES - n_dc)))
    return pl.pallas_call(
        _na_bias_kernel,
        out_shape=jax.ShapeDtypeStruct((3, heads, NA_GROUP_TOK, NA_KEY_TOK), F32),
        grid=(heads,),
        in_specs=[pl.BlockSpec((1, n_dr, LANES), lambda h: (h, 0, 0))],
        out_specs=pl.BlockSpec((3, 1, NA_GROUP_TOK, NA_KEY_TOK), lambda h: (0, h, 0, 0)),
        compiler_params=pltpu.CompilerParams(dimension_semantics=("arbitrary",),
                                             vmem_limit_bytes=VMEM_LIMIT),
        name="na_bias",
    )(w)


def _na(q, k, v, bias, batch, seq_len):
    n_tok = q.shape[0]
    grid_rows = seq_len // GRID_W
    nb = seq_len // NA_TILE
    per = NA_TILE // NA_GROUP_TOK
    nh = seq_len // NA_GROUP_TOK
    cur = lambda b, r: (b * nb + r, 0)
    prv = lambda b, r: (b * nh + jnp.maximum(r * per - 1, 0), 0)
    nxt = lambda b, r: (b * nh + jnp.minimum((r + 1) * per, nh - 1), 0)
    blk = lambda f: pl.BlockSpec((NA_TILE, NA_WIDTH), f)
    halo = lambda f: pl.BlockSpec((NA_GROUP_TOK, NA_WIDTH), f)
    return pl.pallas_call(
        functools.partial(_na_kernel, grid_rows=grid_rows),
        out_shape=jax.ShapeDtypeStruct((n_tok, NA_WIDTH), BF16),
        grid=(batch, nb),
        in_specs=[blk(cur), halo(prv), blk(cur), halo(nxt), halo(prv), blk(cur), halo(nxt),
                  _const_spec(bias.shape)],
        out_specs=blk(cur),
        scratch_shapes=[pltpu.VMEM((NA_TILE + 2 * NA_GROUP_TOK, NA_WIDTH), BF16),
                        pltpu.VMEM((NA_TILE + 2 * NA_GROUP_TOK, NA_WIDTH), BF16)],
        compiler_params=pltpu.CompilerParams(dimension_semantics=("arbitrary", "arbitrary"),
                                             vmem_limit_bytes=VMEM_LIMIT),
        name="na",
    )(q, k, k, k, v, v, v, bias)


def _ret_kernel(q_ref, k_ref, v_ref, g_ref, lgf_ref, lgb_ref, nw_ref, o_ref, t_all, state,
                *, steps):
    phase = pl.program_id(1)
    s = pl.program_id(2)
    chunks = RET_TILE // RET_CHUNK
    C = RET_CHUNK

    @pl.when(s == 0)
    def _():
        state[...] = jnp.zeros_like(state)

    row = lax.broadcasted_iota(jnp.int32, (C, LANES), 0).astype(F32)
    col = lax.broadcasted_iota(jnp.int32, (C, LANES), 1).astype(F32)

    @pl.when(phase == 0)
    def _():
        tile = steps - 1 - s
        heads = [slice(hh * LANES, (hh + 1) * LANES) for hh in range(RET_HEADS)]
        k_decay = [jnp.exp(lgb_ref[:, sl] * row) for sl in heads]
        chunk_decay = [jnp.exp(lgb_ref[:, sl] * C) for sl in heads]
        for c in reversed(range(chunks)):
            rows = slice(c * C, (c + 1) * C)
            for hh, sl in enumerate(heads):
                t_all[tile * chunks + c, hh] = state[hh].astype(BF16)
                kd = (k_ref[rows, sl].astype(F32) * k_decay[hh]).T.astype(BF16)
                kv = jnp.dot(kd, v_ref[rows, sl], preferred_element_type=F32)
                state[hh] = state[hh] * chunk_decay[hh] + kv

    @pl.when(phase == 1)
    def _():
        diff = row - col
        heads = [slice(hh * LANES, (hh + 1) * LANES) for hh in range(RET_HEADS)]
        decay = [jnp.where(diff >= 0, jnp.exp(lgf_ref[:, sl] * jnp.maximum(diff, 0.0)),
                           jnp.exp(lgb_ref[:, sl] * jnp.maximum(-diff, 0.0))) for sl in heads]
        q_decay_f = [jnp.exp(lgf_ref[:, sl] * (row + 1.0)) for sl in heads]
        q_decay_b = [jnp.exp(lgb_ref[:, sl] * (C - row)) for sl in heads]
        k_decay = [jnp.exp(lgf_ref[:, sl] * (C - 1.0 - row)) for sl in heads]
        chunk_decay = [jnp.exp(lgf_ref[:, sl] * C) for sl in heads]
        for c in range(chunks):
            rows = slice(c * C, (c + 1) * C)
            for hh, sl in enumerate(heads):
                q = q_ref[rows, sl]
                k = k_ref[rows, sl]
                v = v_ref[rows, sl]
                a = lax.dot_general(q, k, (((1,), (1,)), ((), ())), preferred_element_type=F32)
                y = jnp.dot((a * decay[hh]).astype(BF16), v, preferred_element_type=F32)
                y = y + q_decay_f[hh] * jnp.dot(q, state[hh].astype(BF16), preferred_element_type=F32)
                y = y + q_decay_b[hh] * jnp.dot(q, t_all[s * chunks + c, hh],
                                                preferred_element_type=F32)
                kd = (k.astype(F32) * k_decay[hh]).T.astype(BF16)
                kv = jnp.dot(kd, v, preferred_element_type=F32)
                state[hh] = state[hh] * chunk_decay[hh] + kv
                g = g_ref[rows, sl].astype(F32)
                gate = g * (1.0 / (1.0 + jnp.exp(-g)))
                o_ref[rows, sl] = (gate * _rms(y, nw_ref[:, sl])).astype(BF16)


def _retention(q, k, v, g, lgf, lgb, nw, batch, seq_len):
    n_tok = q.shape[0]
    steps = seq_len // RET_TILE
    n_chunks = seq_len // RET_CHUNK
    fwd_only = lambda b, p, s: (b * steps + s * p, 0)
    both = lambda b, p, s: (b * steps + jnp.where(p == 0, steps - 1 - s, s), 0)
    blk = lambda f: pl.BlockSpec((RET_TILE, RET_WIDTH), f)
    return pl.pallas_call(
        functools.partial(_ret_kernel, steps=steps),
        out_shape=jax.ShapeDtypeStruct((n_tok, RET_WIDTH), BF16),
        grid=(batch, 2, steps),
        in_specs=[blk(fwd_only), blk(both), blk(both), blk(fwd_only),
                  _const_spec(lgf.shape), _const_spec(lgb.shape), _const_spec(nw.shape)],
        out_specs=blk(fwd_only),
        scratch_shapes=[pltpu.VMEM((n_chunks, RET_HEADS, RET_HEAD_DIM, RET_HEAD_DIM), BF16),
                        pltpu.VMEM((RET_HEADS, RET_HEAD_DIM, RET_HEAD_DIM), F32)],
        compiler_params=pltpu.CompilerParams(
            dimension_semantics=("arbitrary", "arbitrary", "arbitrary"),
            vmem_limit_bytes=VMEM_LIMIT),
        name="retention",
    )(q, k, v, g, lgf, lgb, nw)


def _out_ffn_kernel(x_ref, xp_ref, xn_ref, na_ref, nap_ref, nan_ref, rt_ref, rtp_ref, rtn_ref,
                    wo_ref, n2_ref, wu_ref, cw_ref, cb_ref, wd_ref, o_ref, n_scr, f_scr, act_scr,
                    *, tiles_per_seq, ffn_dim):
    i = pl.program_id(0)
    rows = TOK_TILE + 2 * HALO
    slabs = n_scr.shape[0]
    x = jnp.concatenate([xp_ref[...], x_ref[...], xn_ref[...]], axis=0)

    def with_halo(prev_ref, cur_ref, next_ref):
        return jnp.concatenate([prev_ref[...].astype(F32)[BF16_ROWS - HALO:], cur_ref[...].astype(F32),
                                next_ref[...].astype(F32)[:HALO]], axis=0).astype(BF16)

    mix = jnp.concatenate([with_halo(nap_ref, na_ref, nan_ref), with_halo(rtp_ref, rt_ref, rtn_ref)],
                          axis=1)
    x2 = x + jnp.dot(mix, wo_ref[...], preferred_element_type=F32)
    pos = i % tiles_per_seq
    ridx = lax.broadcasted_iota(jnp.int32, (rows, 1), 0)
    dead = ((ridx < HALO) & (pos == 0)) | ((ridx >= HALO + TOK_TILE) & (pos == tiles_per_seq - 1))
    nf = jnp.where(dead, 0.0, _rms(x2, n2_ref[...]))

    stride = rows // SUBLANES
    for j in range(slabs):
        n_scr[j] = nf[:, j * LANES:(j + 1) * LANES]
    n = jnp.concatenate(
        [jnp.concatenate([n_scr.at[j][pl.ds(v, SUBLANES, stride=stride), :] for j in range(slabs)], axis=1)
         for v in range(stride)], axis=0).astype(BF16)

    n_chunks = ffn_dim // FFN_CHUNK

    def up_proj(c):
        return [jnp.dot(n, wu_ref[:, base + c * FFN_CHUNK:base + (c + 1) * FFN_CHUNK],
                        preferred_element_type=F32) for base in (0, ffn_dim)]

    def conv(h, c, base):
        cols = slice(base + c * FFN_CHUNK, base + (c + 1) * FFN_CHUNK)
        prev = jnp.concatenate([pltpu.roll(h[rows - SUBLANES:], 1, 0), h[:rows - SUBLANES]], axis=0)
        nxt = jnp.concatenate([h[SUBLANES:], pltpu.roll(h[:SUBLANES], SUBLANES - 1, 0)], axis=0)
        return prev * cw_ref[0:1, cols] + h * cw_ref[1:2, cols] + nxt * cw_ref[2:3, cols] + cb_ref[:, cols]

    for c in range(n_chunks):
        h_gate, h_up = up_proj(c)
        gate = conv(h_gate, c, 0)
        up = conv(h_up, c, ffn_dim)
        act_scr[:, c * FFN_CHUNK:(c + 1) * FFN_CHUNK] = (
            gate * (1.0 / (1.0 + jnp.exp(-gate))) * up).astype(BF16)
    ffn = jnp.dot(act_scr[...], wd_ref[...], preferred_element_type=F32)

    for v in range(stride):
        for j in range(slabs):
            f_scr.at[j][pl.ds(v, SUBLANES, stride=stride), :] = (
                ffn[v * SUBLANES:(v + 1) * SUBLANES, j * LANES:(j + 1) * LANES])
    for j in range(slabs):
        cols = slice(j * LANES, (j + 1) * LANES)
        o_ref[:, cols] = f_scr[j, HALO:HALO + TOK_TILE, :] + x2[HALO:HALO + TOK_TILE, cols]


def _out_ffn(x2d, na, rt, seq_len, wo, n2, wu, cw, cb, wd):
    n_tok, d_model = x2d.shape
    ffn_dim = wd.shape[0]
    tiles_per_seq = seq_len // TOK_TILE
    cur = lambda i: (i, 0)
    main = lambda w: pl.BlockSpec((TOK_TILE, w), cur)

    def halo_specs(width, block_rows):
        per_tile = TOK_TILE // block_rows
        last = n_tok // block_rows - 1
        return (pl.BlockSpec((block_rows, width), lambda i: (jnp.maximum(i * per_tile - 1, 0), 0)),
                pl.BlockSpec((block_rows, width), lambda i: (jnp.minimum((i + 1) * per_tile, last), 0)))

    return pl.pallas_call(
        functools.partial(_out_ffn_kernel, tiles_per_seq=tiles_per_seq, ffn_dim=ffn_dim),
        out_shape=jax.ShapeDtypeStruct((n_tok, d_model), F32),
        grid=(n_tok // TOK_TILE,),
        in_specs=[main(d_model), *halo_specs(d_model, HALO),
                  main(NA_WIDTH), *halo_specs(NA_WIDTH, BF16_ROWS),
                  main(RET_WIDTH), *halo_specs(RET_WIDTH, BF16_ROWS),
                  _const_spec(wo.shape), _const_spec(n2.shape), _const_spec(wu.shape),
                  _const_spec(cw.shape), _const_spec(cb.shape), _const_spec(wd.shape)],
        out_specs=main(d_model),
        scratch_shapes=[pltpu.VMEM((d_model // LANES, TOK_TILE + 2 * HALO, LANES), F32),
                        pltpu.VMEM((d_model // LANES, TOK_TILE + 2 * HALO, LANES), F32),
                        pltpu.VMEM((TOK_TILE + 2 * HALO, ffn_dim), BF16)],
        compiler_params=pltpu.CompilerParams(dimension_semantics=("arbitrary",),
                                             vmem_limit_bytes=VMEM_LIMIT),
        name="out_ffn",
    )(x2d, x2d, x2d, na, na, na, rt, rt, rt, wo, n2, wu, cw, cb, wd)


def _rotary_tables():
    half = RET_HEAD_DIM // 2
    inv_freq = ROPE_BASE ** (-jnp.arange(half, dtype=F32) / half)
    freq = jnp.concatenate([inv_freq, inv_freq])[None, :]
    ang = jnp.arange(IN_TILE, dtype=F32)[:, None] * freq
    return freq, jnp.cos(ang), jnp.sin(ang)


def _layer(x, p):
    batch, seq_len, d_model = x.shape
    x2d = x.reshape(batch * seq_len, d_model)
    naq, nak, nav, rq, rk, rv, rg = _in_proj(
        x2d, seq_len, p["n1"], p["w_in"], p["qn"], p["kn"], p["rope"])
    na = _na(naq, nak, nav, p["bias"], batch, seq_len)
    rt = _retention(rq, rk, rv, rg, p["lgf"], p["lgb"], p["nw"], batch, seq_len)
    y = _out_ffn(x2d, na, rt, seq_len, p["wo"], p["n2"], p["wu"], p["cw"], p["cb"], p["wd"])
    return y.reshape(batch, seq_len, d_model)


def _prepare(norm1_w, w_in, na_q_norm_w, na_k_norm_w, na_rpb, ret_theta_fwd, ret_theta_bwd,
             ret_norm_w, w_out, norm2_w, w_ffn_up, ffn_conv_w, ffn_conv_b, w_ffn_down):
    lanes = lambda t: jnp.repeat(t.astype(F32), RET_HEAD_DIM)[None, :]
    return dict(
        n1=norm1_w[None, :], w_in=w_in.astype(BF16), rope=_rotary_tables(),
        qn=jnp.tile(na_q_norm_w, NA_HEADS)[None, :], kn=jnp.tile(na_k_norm_w, NA_HEADS)[None, :],
        bias=_na_bias_tables(na_rpb).reshape(3, NA_HEADS // 2, 2 * NA_GROUP_TOK, NA_KEY_TOK),
        lgf=lanes(jax.nn.log_sigmoid(ret_theta_fwd.astype(F32))),
        lgb=lanes(jax.nn.log_sigmoid(ret_theta_bwd.astype(F32))),
        nw=ret_norm_w[None, :], wo=w_out.astype(BF16), n2=norm2_w[None, :],
        wu=w_ffn_up.astype(BF16), cw=ffn_conv_w, cb=ffn_conv_b[None, :], wd=w_ffn_down.astype(BF16))


def kernel(x_prompt, x_sample, norm1_w, w_in, na_q_norm_w, na_k_norm_w, na_rpb, ret_theta_fwd, ret_theta_bwd, ret_norm_w, w_out, norm2_w, w_ffn_up, ffn_conv_w, ffn_conv_b, w_ffn_down):
    depth = norm1_w.shape[0]
    y_prompt, y_sample = x_prompt, x_sample
    for l in range(depth):
        p = _prepare(norm1_w[l], w_in[l], na_q_norm_w[l], na_k_norm_w[l], na_rpb[l],
                     ret_theta_fwd[l], ret_theta_bwd[l], ret_norm_w[l], w_out[l], norm2_w[l],
                     w_ffn_up[l], ffn_conv_w[l], ffn_conv_b[l], w_ffn_down[l])
        y_prompt = _layer(y_prompt, p)
        y_sample = _layer(y_sample, p)
    return (y_prompt, y_sample)
```
